```python
import jax
import jax.numpy as jnp
from jax import lax
import numpy as np

D_MODEL = 2048
BATCH = 4
SEQ = 2048
DEPTH = 1
DEC_BATCH = 128
DEC_SEQ = 1
PAST_LEN = 16384
PAGE_SIZE = 128

CONV_W = 4
SSD_HEAD_DIM = 64
SSD_INNER = D_MODEL // 2
SSD_HEADS = SSD_INNER // SSD_HEAD_DIM
SSD_GROUPS = 2
SSD_HPG = SSD_HEADS // SSD_GROUPS
SSD_STATE = 128
SSD_CHUNK = 128
SSD_CONV_DIM = SSD_INNER + 2 * SSD_GROUPS * SSD_STATE
GDN_DK = 128
GDN_DV = 128
GDN_VDIM = D_MODEL // 2
GDN_HEADS = GDN_VDIM // GDN_DV
GDN_CHUNK = 64
GDN_CONV_DIM = GDN_HEADS * (2 * GDN_DK + GDN_DV)
D_MIX = SSD_INNER + GDN_VDIM
IN_SPLITS = (SSD_INNER, SSD_CONV_DIM, SSD_HEADS, GDN_CONV_DIM, GDN_HEADS, GDN_HEADS, GDN_VDIM)
IN_DIM = SSD_INNER + SSD_CONV_DIM + SSD_HEADS + GDN_CONV_DIM + 2 * GDN_HEADS + GDN_VDIM
PEER_HEADS = 8
PEER_NKEYS = 128
PEER_EXPERTS = PEER_NKEYS * PEER_NKEYS
PEER_DKEY = 256
PEER_TOPK = 16
PEER_BLOCK = 128
EPS = 1e-6

kernel_name = "hymba_ssd_gdn_peer_adaln_step"


def _split(a, sizes):
    offs = [int(o) for o in np.cumsum(sizes)[:-1]]
    return jnp.split(a, offs, axis=-1)


def rmsnorm(x, w):
    xf = x.astype(jnp.float32)
    y = xf * lax.rsqrt(jnp.mean(xf * xf, axis=-1, keepdims=True) + EPS)
    return (y * w.astype(jnp.float32)).astype(x.dtype)


def l2norm(x):
    return x * lax.rsqrt(jnp.sum(x * x, axis=-1, keepdims=True) + EPS)


def causal_conv(x, buf, w):
    L = x.shape[1]
    xp = jnp.concatenate([buf.astype(x.dtype), x], axis=1)
    y = xp[:, 0:L] * w[0]
    for k in range(1, CONV_W):
        y = y + xp[:, k:k + L] * w[k]
    return y, xp[:, L:]


def to_chunks(a, q):
    n, L = a.shape[:2]
    nc = -(-L // q)
    a = jnp.pad(a, [(0, 0), (0, nc * q - L)] + [(0, 0)] * (a.ndim - 2))
    return jnp.moveaxis(a.reshape((n, nc, q) + a.shape[2:]), 1, 0)


def from_chunks(a, L):
    nc, n, q = a.shape[:3]
    return jnp.moveaxis(a, 0, 1).reshape((n, nc * q) + a.shape[3:])[:, :L]


def ssd_scan(x, dt, A, Bm, Cm, h0):
    n, L = x.shape[:2]
    q = min(SSD_CHUNK, L)
    G, R, P, N = SSD_GROUPS, SSD_HPG, SSD_HEAD_DIM, SSD_STATE
    xs = to_chunks(x.reshape(n, L, G, R, P), q)
    dts = to_chunks(dt.reshape(n, L, G, R), q)
    Bs = to_chunks(Bm, q)
    Cs = to_chunks(Cm, q)
    Ag = A.reshape(G, R)
    causal = jnp.tril(jnp.ones((q, q), dtype=bool))[None, :, :, None, None]

    def step(h, inp):
        xc, dtc, Bc, Cc = inp
        cs = jnp.cumsum(dtc * Ag, axis=1)
        decay = jnp.exp(jnp.where(causal, cs[:, :, None] - cs[:, None, :], -jnp.inf))
        cb = jnp.einsum('bign,bjgn->bijg', Cc, Bc)
        y = jnp.einsum('bijg,bijgr,bjgrp->bigrp', cb, decay, xc * dtc[..., None])
        y = y + jnp.einsum('bign,bgrpn->bigrp', Cc, h) * jnp.exp(cs)[..., None]
        last = cs[:, -1]
        h = h * jnp.exp(last)[..., None, None] + jnp.einsum(
            'bjgn,bjgrp->bgrpn', Bc, xc * (dtc * jnp.exp(last[:, None] - cs))[..., None])
        return h, y

    hT, ys = lax.scan(step, h0.reshape(n, G, R, P, N), (xs, dts, Bs, Cs))
    return from_chunks(ys, L).reshape(n, L, SSD_HEADS, P), hT.reshape(n, SSD_HEADS, P, N)


def gdn_scan(q, k, v, g, beta, S0):
    n, L = q.shape[:2]
    c = min(GDN_CHUNK, L)
    tc = lambda a: jnp.swapaxes(to_chunks(a, c), 2, 3)
    qs, ks, vs, gs, bs = tc(q), tc(k), tc(v), tc(g), tc(beta)
    incl = jnp.tril(jnp.ones((c, c), dtype=bool))
    strict = jnp.tril(jnp.ones((c, c), dtype=bool), -1)
    eye = jnp.eye(c, dtype=jnp.float32)

    def step(S, inp):
        qc, kc, vc, gc, bc = inp
        gcum = jnp.cumsum(gc, axis=-1)
        decay = jnp.exp(jnp.where(incl, gcum[..., :, None] - gcum[..., None, :], -jnp.inf))
        kb = kc * bc[..., None]
        lower = jnp.where(strict, jnp.einsum('bhik,bhjk->bhij', kb, kc) * decay, 0.0)
        rhs = jnp.concatenate([vc * bc[..., None], kb * jnp.exp(gcum)[..., None]], axis=-1)
        sol = lax.linalg.triangular_solve(lower + eye, rhs, left_side=True, lower=True, unit_diagonal=True)
        u, w = sol[..., :GDN_DV], sol[..., GDN_DV:]
        v_new = u - jnp.einsum('bhik,bhkv->bhiv', w, S)
        qk = jnp.einsum('bhik,bhjk->bhij', qc, kc) * decay
        o = jnp.einsum('bhik,bhkv->bhiv', qc * jnp.exp(gcum)[..., None], S) + jnp.einsum('bhij,bhjv->bhiv', qk, v_new)
        last = gcum[..., -1]
        S = S * jnp.exp(last)[..., None, None] + jnp.einsum(
            'bhik,bhiv->bhkv', kc * jnp.exp(last[..., None] - gcum)[..., None], v_new)
        return S, o

    ST, outs = lax.scan(step, S0, (qs, ks, vs, gs, bs))
    return from_chunks(jnp.swapaxes(outs, 2, 3), L), ST


def mixer(h, ssd_buf, gdn_buf, S_ssd, S_gdn, w_in, ssd_conv_w, ssd_conv_b, ssd_dt_bias, ssd_A_log, ssd_D,
          ssd_norm_w, gdn_conv_w, gdn_dt_bias, gdn_A_log, gdn_norm_w, w_out):
    n, L, _ = h.shape
    f32 = jnp.float32
    z, xbc, dt_raw, qkv, a_raw, b_raw, gate = _split(h @ w_in, IN_SPLITS)
    xbc, ssd_buf_new = causal_conv(xbc, ssd_buf, ssd_conv_w)
    xbc = jax.nn.silu(xbc + ssd_conv_b).astype(f32)
    xs, Bm, Cm = _split(xbc, (SSD_INNER, SSD_GROUPS * SSD_STATE, SSD_GROUPS * SSD_STATE))
    xs = xs.reshape(n, L, SSD_HEADS, SSD_HEAD_DIM)
    dt = jax.nn.softplus(dt_raw.astype(f32) + ssd_dt_bias.astype(f32))
    A = -jnp.exp(ssd_A_log.astype(f32))
    y, S_ssd_new = ssd_scan(xs, dt, A, Bm.reshape(n, L, SSD_GROUPS, SSD_STATE),
                            Cm.reshape(n, L, SSD_GROUPS, SSD_STATE), S_ssd.astype(f32))
    y = y + ssd_D.astype(f32)[:, None] * xs
    y = y.reshape(n, L, SSD_INNER) * jax.nn.silu(z.astype(f32))
    y = rmsnorm(y.reshape(n, L, SSD_GROUPS, SSD_INNER // SSD_GROUPS),
                ssd_norm_w.reshape(SSD_GROUPS, SSD_INNER // SSD_GROUPS)).reshape(n, L, SSD_INNER)
    qkv, gdn_buf_new = causal_conv(qkv, gdn_buf, gdn_conv_w)
    qkv = jax.nn.silu(qkv).astype(f32)
    q, k, v = _split(qkv, (GDN_HEADS * GDN_DK, GDN_HEADS * GDN_DK, GDN_VDIM))
    q = l2norm(q.reshape(n, L, GDN_HEADS, GDN_DK)) * (GDN_DK ** -0.5)
    k = l2norm(k.reshape(n, L, GDN_HEADS, GDN_DK))
    v = v.reshape(n, L, GDN_HEADS, GDN_DV)
    beta = jax.nn.sigmoid(b_raw.astype(f32))
    g = -jnp.exp(gdn_A_log.astype(f32)) * jax.nn.softplus(a_raw.astype(f32) + gdn_dt_bias.astype(f32))
    o, S_gdn_new = gdn_scan(q, k, v, g, beta, S_gdn.astype(f32))
    o = rmsnorm(o, gdn_norm_w) * jax.nn.silu(gate.astype(f32)).reshape(n, L, GDN_HEADS, GDN_DV)
    mixed = jnp.concatenate([y, o.reshape(n, L, GDN_VDIM)], axis=-1).astype(h.dtype)
    states = (S_ssd_new.astype(h.dtype), ssd_buf_new, S_gdn_new.astype(h.dtype), gdn_buf_new)
    return mixed @ w_out, states


def peer(h, w_q, sub_keys, u_tab, v_tab):
    n, L, D = h.shape
    T = n * L
    Tp = -(-T // PEER_BLOCK) * PEER_BLOCK
    blocks = jnp.pad(h.reshape(T, D), ((0, Tp - T), (0, 0))).reshape(Tp // PEER_BLOCK, PEER_BLOCK, D)
    K = PEER_TOPK

    def one(xb):
        tb = xb.shape[0]
        qh = (xb @ w_q).reshape(tb, PEER_HEADS, 2, PEER_DKEY // 2)
        s = jnp.einsum('thsd,hskd->thsk', qh, sub_keys).astype(jnp.float32)
        sv, si = lax.top_k(s, K)
        cand = (sv[:, :, 0, :, None] + sv[:, :, 1, None, :]).reshape(tb, PEER_HEADS, K * K)
        cv, ci = lax.top_k(cand, K)
        i1 = jnp.take_along_axis(si[:, :, 0], ci // K, axis=-1)
        i2 = jnp.take_along_axis(si[:, :, 1], ci % K, axis=-1)
        eid = i1 * PEER_NKEYS + i2
        gate = jax.nn.softmax(cv, axis=-1)
        a = jnp.einsum('td,thkd->thk', xb, u_tab[eid])
        coef = (gate * jax.nn.gelu(a.astype(jnp.float32), approximate=False)).astype(xb.dtype)
        return jnp.einsum('thk,thkd->td', coef, v_tab[eid])

    out = lax.map(one, blocks)
    return out.reshape(Tp, D)[:T].reshape(n, L, D)


def layer(x, c, st_ssd, st_ssd_conv, st_gdn, st_gdn_conv, p):
    (w_ada, b_ada, norm1_w, norm2_w, w_in, ssd_conv_w, ssd_conv_b, ssd_dt_bias, ssd_A_log, ssd_D, ssd_norm_w,
     gdn_conv_w, gdn_dt_bias, gdn_A_log, gdn_norm_w, w_out, peer_w_q, peer_sub_keys, peer_u, peer_v) = p
    mod = (jax.nn.silu(c) @ w_ada + b_ada)[:, None, :]
    sh1, sc1, g1, sh2, sc2, g2 = jnp.split(mod, 6, axis=-1)
    hm = rmsnorm(x, norm1_w) * (1 + sc1) + sh1
    m, states = mixer(hm, st_ssd_conv, st_gdn_conv, st_ssd, st_gdn, w_in, ssd_conv_w, ssd_conv_b, ssd_dt_bias,
                      ssd_A_log, ssd_D, ssd_norm_w, gdn_conv_w, gdn_dt_bias, gdn_A_log, gdn_norm_w, w_out)
    x = x + g1 * m
    hf = rmsnorm(x, norm2_w) * (1 + sc2) + sh2
    x = x + g2 * peer(hf, peer_w_q, peer_sub_keys, peer_u, peer_v)
    return x, states


def trunk(x, c, st_ssd, st_ssd_conv, st_gdn, st_gdn_conv, params, final_norm_w):
    new = ([], [], [], [])
    for l in range(DEPTH):
        p = tuple(w[l] for w in params)
        x, states = layer(x, c, st_ssd[l], st_ssd_conv[l], st_gdn[l], st_gdn_conv[l], p)
        for lst, s in zip(new, states):
            lst.append(s)
    return rmsnorm(x, final_norm_w), [jnp.stack(s) for s in new]


def setup_inputs(seed: int = 0) -> dict:
    key = jax.random.key(seed)
    ks = jax.random.split(key, 40)
    f32 = jnp.float32
    nrm = lambda k, shape, std: jax.random.normal(k, shape, f32) * std

    def dt_bias(k, h):
        dt0 = jnp.exp(jax.random.uniform(k, (DEPTH, h), f32, minval=np.log(1e-3), maxval=np.log(1e-1)))
        return dt0 + jnp.log(-jnp.expm1(-dt0))

    return {
        "x_prompt": nrm(ks[0], (BATCH, SEQ, D_MODEL), 1.0),
        "x_sample": nrm(ks[1], (DEC_BATCH, DEC_SEQ, D_MODEL), 1.0),
        "c_prompt": nrm(ks[2], (BATCH, D_MODEL), 1.0),
        "c_sample": nrm(ks[3], (DEC_BATCH, D_MODEL), 1.0),
        "state_ssd": nrm(ks[4], (DEPTH, DEC_BATCH, SSD_HEADS, SSD_HEAD_DIM, SSD_STATE), 0.3),
        "state_ssd_conv": nrm(ks[5], (DEPTH, DEC_BATCH, CONV_W - 1, SSD_CONV_DIM), 1.0),
        "state_gdn": nrm(ks[6], (DEPTH, DEC_BATCH, GDN_HEADS, GDN_DK, GDN_DV), 0.1),
        "state_gdn_conv": nrm(ks[7], (DEPTH, DEC_BATCH, CONV_W - 1, GDN_CONV_DIM), 1.0),
        "w_ada": nrm(ks[8], (DEPTH, D_MODEL, 6 * D_MODEL), 0.5 * D_MODEL ** -0.5),
        "b_ada": nrm(ks[9], (DEPTH, 6 * D_MODEL), 0.02),
        "norm1_w": 1.0 + nrm(ks[10], (DEPTH, D_MODEL), 0.02),
        "norm2_w": 1.0 + nrm(ks[11], (DEPTH, D_MODEL), 0.02),
        "w_in": nrm(ks[12], (DEPTH, D_MODEL, IN_DIM), D_MODEL ** -0.5),
        "ssd_conv_w": nrm(ks[13], (DEPTH, CONV_W, SSD_CONV_DIM), CONV_W ** -0.5),
        "ssd_conv_b": nrm(ks[14], (DEPTH, SSD_CONV_DIM), 0.02),
        "ssd_dt_bias": dt_bias(ks[15], SSD_HEADS),
        "ssd_A_log": jnp.log(jax.random.uniform(ks[16], (DEPTH, SSD_HEADS), f32, minval=1.0, maxval=16.0)),
        "ssd_D": 1.0 + nrm(ks[17], (DEPTH, SSD_HEADS), 0.02),
        "ssd_norm_w": 1.0 + nrm(ks[18], (DEPTH, SSD_INNER), 0.02),
        "gdn_conv_w": nrm(ks[19], (DEPTH, CONV_W, GDN_CONV_DIM), CONV_W ** -0.5),
        "gdn_dt_bias": dt_bias(ks[20], GDN_HEADS),
        "gdn_A_log": jnp.log(jax.random.uniform(ks[21], (DEPTH, GDN_HEADS), f32, minval=1.0, maxval=16.0)),
        "gdn_norm_w": 1.0 + nrm(ks[22], (DEPTH, GDN_DV), 0.02),
        "w_out": nrm(ks[23], (DEPTH, D_MIX, D_MODEL), D_MIX ** -0.5),
        "peer_w_q": nrm(ks[24], (DEPTH, D_MODEL, PEER_HEADS * PEER_DKEY), D_MODEL ** -0.5),
        "peer_sub_keys": nrm(ks[25], (DEPTH, PEER_HEADS, 2, PEER_NKEYS, PEER_DKEY // 2), (PEER_DKEY // 2) ** -0.5),
        "peer_u": nrm(ks[26], (DEPTH, PEER_EXPERTS, D_MODEL), D_MODEL ** -0.5),
        "peer_v": nrm(ks[27], (DEPTH, PEER_EXPERTS, D_MODEL), PEER_HEADS ** -0.5),
        "final_norm_w": 1.0 + nrm(ks[28], (D_MODEL,), 0.02),
    }


def reference(x_prompt, x_sample, c_prompt, c_sample, state_ssd, state_ssd_conv, state_gdn, state_gdn_conv,
              w_ada, b_ada, norm1_w, norm2_w, w_in, ssd_conv_w, ssd_conv_b, ssd_dt_bias, ssd_A_log, ssd_D,
              ssd_norm_w, gdn_conv_w, gdn_dt_bias, gdn_A_log, gdn_norm_w, w_out, peer_w_q, peer_sub_keys,
              peer_u, peer_v, final_norm_w):
    params = (w_ada, b_ada, norm1_w, norm2_w, w_in, ssd_conv_w, ssd_conv_b, ssd_dt_bias, ssd_A_log, ssd_D,
              ssd_norm_w, gdn_conv_w, gdn_dt_bias, gdn_A_log, gdn_norm_w, w_out, peer_w_q, peer_sub_keys,
              peer_u, peer_v)
    nb = x_prompt.shape[0]
    dt = x_prompt.dtype
    zs = jnp.zeros((DEPTH, nb, SSD_HEADS, SSD_HEAD_DIM, SSD_STATE), dt)
    zsc = jnp.zeros((DEPTH, nb, CONV_W - 1, SSD_CONV_DIM), dt)
    zg = jnp.zeros((DEPTH, nb, GDN_HEADS, GDN_DK, GDN_DV), dt)
    zgc = jnp.zeros((DEPTH, nb, CONV_W - 1, GDN_CONV_DIM), dt)
    y_prompt, sp = trunk(x_prompt, c_prompt, zs, zsc, zg, zgc, params, final_norm_w)
    y_sample, ss = trunk(x_sample, c_sample, state_ssd, state_ssd_conv, state_gdn, state_gdn_conv, params, final_norm_w)
    return (y_prompt, y_sample, sp[0], sp[1], sp[2], sp[3], ss[0], ss[1], ss[2], ss[3])
```

```python
import functools
import math

import jax
import jax.numpy as jnp
from jax import lax
from jax.experimental import pallas as pl
from jax.experimental.pallas import tpu as pltpu

f32 = jnp.float32
bf16 = jnp.bfloat16

D = 2048
CONV_W = 4
SSD_P = 64
SSD_INNER = D // 2
SSD_H = SSD_INNER // SSD_P
SSD_G = 2
SSD_N = 128
SSD_CHUNK = 128
SSD_BC = 2 * SSD_G * SSD_N
SSD_CONV = SSD_INNER + SSD_BC
GDN_DK = 128
GDN_DV = 128
GDN_V = D // 2
GDN_H = GDN_V // GDN_DV
GDN_CHUNK = 64
GDN_CONV = GDN_H * (2 * GDN_DK + GDN_DV)
PEER_H = 8
PEER_NK = 128
PEER_E = PEER_NK * PEER_NK
PEER_DK = 256
PEER_K = 16
EPS = 1e-6
NEG = -1e30

OFF_Z, OFF_XS, OFF_GATE, OFF_QKV, OFF_BC, OFF_SM = 0, 1024, 2048, 3072, 6144, 6656
P_COLS = 7168
SM_DT, SM_A, SM_B = 0, 16, 24

VMEM_LIMIT = 56 * 1024 * 1024


def _cp(sem):
    return pltpu.CompilerParams(dimension_semantics=sem, vmem_limit_bytes=VMEM_LIMIT)


def _silu(x):
    return x * jax.nn.sigmoid(x)


def _softplus(x):
    return jnp.maximum(x, 0.0) + jnp.log1p(jnp.exp(-jnp.abs(x)))


def _dot(a, b):
    return jnp.dot(a, b, preferred_element_type=f32)


def _dot_nt(a, b):
    return lax.dot_general(a, b, (((1,), (1,)), ((), ())), preferred_element_type=f32)


def _dot_hi(a, b):
    return jnp.dot(a, b, preferred_element_type=f32, precision=lax.Precision.HIGHEST)


def _split_bf16(a):
    hi = a.astype(bf16)
    lo = (a - hi.astype(f32)).astype(bf16)
    return hi, lo


def _dot3(a, b):
    ah, al = _split_bf16(a)
    bh, bl = _split_bf16(b)
    return _dot(ah, bh) + (_dot(ah, bl) + _dot(al, bh))


def _iota(shape, dim):
    return lax.broadcasted_iota(jnp.int32, shape, dim)


def _mod_kernel(c_ref, w_ref, b_ref, o_ref):
    a = _silu(c_ref[...]).astype(bf16)
    o_ref[...] = _dot(a, w_ref[...].astype(bf16)) + b_ref[...]


def _modulation(c_all, w_ada, b_ada):
    m, tn = c_all.shape[0], 1024
    return pl.pallas_call(
        _mod_kernel,
        grid=(6 * D // tn,),
        in_specs=[pl.BlockSpec((m, D), lambda j: (0, 0)),
                  pl.BlockSpec((D, tn), lambda j: (0, j)),
                  pl.BlockSpec((1, tn), lambda j: (0, j))],
        out_specs=pl.BlockSpec((m, tn), lambda j: (0, j)),
        out_shape=jax.ShapeDtypeStruct((m, 6 * D), f32),
        compiler_params=_cp(("arbitrary",)),
        name="adaln_mod",
    )(c_all, w_ada, b_ada.reshape(1, 6 * D))


def _mod_specs(per_token, tm, rows_per_seq, which, grid_rank):
    if per_token:
        if grid_rank == 2:
            return pl.BlockSpec((tm, D), lambda i, j: (i, which))
        return pl.BlockSpec((tm, D), lambda i: (i, which))
    tps = rows_per_seq // tm
    if grid_rank == 2:
        return pl.BlockSpec((None, 1, D), lambda i, j: (i // tps, 0, which))
    return pl.BlockSpec((None, 1, D), lambda i: (i // tps, 0, which))


def _inproj_kernel(x_ref, sh_ref, sc_ref, nw_ref, w_ref, o_ref, hm_ref):
    @pl.when(pl.program_id(1) == 0)
    def _():
        x = x_ref[...]
        y = x * lax.rsqrt(jnp.mean(x * x, axis=-1, keepdims=True) + EPS) * nw_ref[...]
        hm_ref[...] = (y * (1.0 + sc_ref[...]) + sh_ref[...]).astype(bf16)

    o_ref[...] = _dot(hm_ref[...], w_ref[...])


def _inproj(x, mod, norm_w, w_re, per_token, rows_per_seq):
    t = x.shape[0]
    tm = min(512, t)
    tn = 1024
    return pl.pallas_call(
        _inproj_kernel,
        grid=(t // tm, P_COLS // tn),
        in_specs=[pl.BlockSpec((tm, D), lambda i, j: (i, 0)),
                  _mod_specs(per_token, tm, rows_per_seq, 0, 2),
                  _mod_specs(per_token, tm, rows_per_seq, 1, 2),
                  pl.BlockSpec((1, D), lambda i, j: (0, 0)),
                  pl.BlockSpec((D, tn), lambda i, j: (0, j))],
        out_specs=pl.BlockSpec((tm, tn), lambda i, j: (i, j)),
        out_shape=jax.ShapeDtypeStruct((t, P_COLS), f32),
        scratch_shapes=[pltpu.VMEM((tm, D), bf16)],
        compiler_params=_cp(("parallel", "arbitrary")),
        name="inproj",
    )(x, mod, mod, norm_w.reshape(1, D), w_re)


def _conv_chunk(ext_ref, cur, w_ref, b_row, q):
    ext_ref[8:8 + q, :] = cur
    acc = w_ref[CONV_W - 1:CONV_W, :] * cur
    for k in range(CONV_W - 1):
        acc = acc + w_ref[k:k + 1, :] * ext_ref[5 + k:5 + k + q, :]
    ext_ref[0:8, :] = cur[q - 8:q, :]
    if b_row is not None:
        acc = acc + b_row
    return acc


def _group_rmsnorm(y, nw, width):
    outs = []
    for g in range(y.shape[-1] // width):
        seg = y[:, g * width:(g + 1) * width]
        ms = jnp.mean(seg * seg, axis=-1, keepdims=True)
        outs.append(seg * lax.rsqrt(ms + EPS) * nw[:, g * width:(g + 1) * width])
    return jnp.concatenate(outs, axis=-1)


def _lane_mask(shape, lo, hi):
    lane = _iota(shape, len(shape) - 1)
    return (lane >= lo) & (lane < hi)


def _ssd_kernel(z_ref, xs_ref, bc_ref, sm_ref, wxs_ref, wbc_ref, bxs_ref, bbc_ref, dtb_ref, alog_ref, de_ref,
                nw_ref, y_ref, st_ref, extx, extb, ht, yscr):
    q = SSD_CHUNK
    c = pl.program_id(1)

    @pl.when(c == 0)
    def _():
        extx[0:8, :] = jnp.zeros((8, SSD_INNER), f32)
        extb[0:8, :] = jnp.zeros((8, SSD_BC), f32)
        ht[...] = jnp.zeros(ht.shape, f32)

    xs = _silu(_conv_chunk(extx, xs_ref[...], wxs_ref, bxs_ref[...], q))
    bcv = _silu(_conv_chunk(extb, bc_ref[...], wbc_ref, bbc_ref[...], q))

    head_lanes = _lane_mask((q, 128), SM_DT, SM_DT + SSD_H)
    dt = jnp.where(head_lanes, _softplus(sm_ref[...] + dtb_ref[...]), 0.0)
    a_row = -jnp.exp(alog_ref[...])
    tri = (_iota((q, q), 0) >= _iota((q, q), 1))
    cs = _dot_hi(tri.astype(f32), dt * a_row)
    cs_t = cs.T
    dt_t = dt.T
    last_col = cs_t[:, q - 1:q]
    w_t = dt_t * jnp.exp(last_col - cs_t)
    ecs = jnp.exp(cs)
    lane_lo = _iota((q, 128), 1) < SSD_P

    for g in range(SSD_G):
        b_g = bcv[:, g * SSD_N:(g + 1) * SSD_N]
        c_g = bcv[:, SSD_G * SSD_N + g * SSD_N:SSD_G * SSD_N + (g + 1) * SSD_N]
        b_gt = b_g.T
        cb = _dot(c_g.astype(bf16), b_gt.astype(bf16))
        for r2 in range(SSD_H // SSD_G // 2):
            pair = g * (SSD_H // SSD_G // 2) + r2
            xs_pair = xs[:, pair * 128:(pair + 1) * 128]
            xs_pair_b = xs_pair.astype(bf16)
            h_prev = ht[pair]
            rhs = jnp.concatenate([xs_pair_b, h_prev.astype(bf16)], axis=0)
            ys, hs, els = [], [], []
            for e in range(2):
                h = 2 * pair + e
                diff = cs[:, h:h + 1] - cs_t[h:h + 1, :]
                dec = jnp.exp(jnp.where(tri, diff, NEG))
                m = cb * dec * dt_t[h:h + 1, :]
                lhs = jnp.concatenate([m, c_g * ecs[:, h:h + 1]], axis=1).astype(bf16)
                ys.append(_dot(lhs, rhs))
                bw = (b_gt * w_t[h:h + 1, :]).astype(bf16)
                hs.append(_dot(bw, xs_pair_b))
                els.append(jnp.exp(last_col[h:h + 1, :]))
            yscr[:, pair * 128:(pair + 1) * 128] = jnp.where(lane_lo, ys[0], ys[1])
            ht[pair] = h_prev * jnp.where(lane_lo, els[0], els[1]) + jnp.where(lane_lo, hs[0], hs[1])

    z = z_ref[...]
    y = (yscr[...] + de_ref[...] * xs) * _silu(z)
    y_ref[...] = _group_rmsnorm(y, nw_ref[...], SSD_INNER // SSD_G).astype(y_ref.dtype)

    @pl.when(c == pl.num_programs(1) - 1)
    def _():
        for pair in range(SSD_H // 2):
            t = ht[pair].T
            st_ref[2 * pair] = t[0:SSD_P]
            st_ref[2 * pair + 1] = t[SSD_P:2 * SSD_P]


def _ssd_prompt(proj, nb, seq, w):
    nc = seq // SSD_CHUNK
    q = SSD_CHUNK
    row = lambda n, c: n * nc + c
    const = lambda shape: pl.BlockSpec(shape, lambda n, c: (0, 0))
    return pl.pallas_call(
        _ssd_kernel,
        grid=(nb, nc),
        in_specs=[pl.BlockSpec((q, 1024), lambda n, c: (row(n, c), OFF_Z // 1024)),
                  pl.BlockSpec((q, 1024), lambda n, c: (row(n, c), OFF_XS // 1024)),
                  pl.BlockSpec((q, SSD_BC), lambda n, c: (row(n, c), OFF_BC // SSD_BC)),
                  pl.BlockSpec((q, 128), lambda n, c: (row(n, c), OFF_SM // 128)),
                  const((CONV_W, SSD_INNER)), const((CONV_W, SSD_BC)), const((1, SSD_INNER)), const((1, SSD_BC)),
                  const((1, 128)), const((1, 128)), const((1, SSD_INNER)), const((1, SSD_INNER))],
        out_specs=[pl.BlockSpec((q, SSD_INNER), lambda n, c: (row(n, c), 0)),
                   pl.BlockSpec((None, SSD_H, SSD_P, SSD_N), lambda n, c: (n, 0, 0, 0))],
        out_shape=[jax.ShapeDtypeStruct((nb * seq, SSD_INNER), bf16),
                   jax.ShapeDtypeStruct((nb, SSD_H, SSD_P, SSD_N), f32)],
        scratch_shapes=[pltpu.VMEM((q + 8, SSD_INNER), f32), pltpu.VMEM((q + 8, SSD_BC), f32),
                        pltpu.VMEM((SSD_H // 2, SSD_N, 2 * SSD_P), f32), pltpu.VMEM((q, SSD_INNER), f32)],
        compiler_params=_cp(("parallel", "arbitrary")),
        name="ssd_scan",
    )(proj, proj, proj, proj, w["ssd_wxs"], w["ssd_wbc"], w["ssd_bxs"], w["ssd_bbc"], w["ssd_dtb"],
      w["ssd_alog"], w["ssd_de"], w["ssd_nw"])


def _gdn_kernel(qkv_ref, gate_ref, sm_ref, cw_ref, dtb_ref, alog_ref, nw_ref, o_ref, st_ref, ext, s_scr):
    q = GDN_CHUNK
    c = pl.program_id(1)

    @pl.when(c == 0)
    def _():
        ext[0:8, :] = jnp.zeros((8, GDN_CONV), f32)
        s_scr[...] = jnp.zeros(s_scr.shape, f32)

    act = _silu(_conv_chunk(ext, qkv_ref[...], cw_ref, None, q))
    sm = sm_ref[...]
    a_lanes = _lane_mask((q, 128), SM_A, SM_A + GDN_H)
    g = jnp.where(a_lanes, -jnp.exp(alog_ref[...]) * _softplus(sm + dtb_ref[...]), 0.0)
    beta = jax.nn.sigmoid(sm)
    incl = _iota((q, q), 0) >= _iota((q, q), 1)
    strict = _iota((q, q), 0) > _iota((q, q), 1)
    gc = _dot_hi(incl.astype(f32), g)
    gc_t = gc.T
    egc = jnp.exp(gc)
    gate = gate_ref[...]
    nw = nw_ref[...]

    for h in range(GDN_H):
        qh = act[:, h * GDN_DK:(h + 1) * GDN_DK]
        kh = act[:, GDN_H * GDN_DK + h * GDN_DK:GDN_H * GDN_DK + (h + 1) * GDN_DK]
        vh = act[:, 2 * GDN_H * GDN_DK + h * GDN_DV:2 * GDN_H * GDN_DK + (h + 1) * GDN_DV]
        qn = qh * lax.rsqrt(jnp.sum(qh * qh, axis=-1, keepdims=True) + EPS) * (GDN_DK ** -0.5)
        kn = kh * lax.rsqrt(jnp.sum(kh * kh, axis=-1, keepdims=True) + EPS)
        gcol = gc[:, SM_A + h:SM_A + h + 1]
        grow = gc_t[SM_A + h:SM_A + h + 1, :]
        ecol = egc[:, SM_A + h:SM_A + h + 1]
        bcol = beta[:, SM_B + h:SM_B + h + 1]
        dec = jnp.exp(jnp.where(incl, gcol - grow, NEG))
        kb = kn * bcol
        kn_b = kn.astype(bf16)
        kk = _dot_nt(kb.astype(bf16), kn_b)
        n_pow = jnp.where(strict, -(kk * dec), 0.0)
        t_m = n_pow
        for _ in range(5):
            n_pow = _dot3(n_pow, n_pow)
            t_m = t_m + n_pow + _dot3(t_m, n_pow)
        rhs = jnp.concatenate([vh * bcol, kb * ecol], axis=1)
        sol = rhs + _dot3(t_m, rhs)
        u = sol[:, :GDN_DV]
        w = sol[:, GDN_DV:]
        s_prev = s_scr[h]
        s_b = s_prev.astype(bf16)
        v_new = u - _dot(w.astype(bf16), s_b)
        v_new_b = v_new.astype(bf16)
        qk = _dot_nt(qn.astype(bf16), kn_b) * dec
        o = _dot((qn * ecol).astype(bf16), s_b) + _dot(qk.astype(bf16), v_new_b)
        glast = gcol[q - 1:q, :]
        kdec = kn * jnp.exp(glast - gcol)
        s_scr[h] = s_prev * jnp.exp(glast) + _dot(kdec.T.astype(bf16), v_new_b)
        on = o * lax.rsqrt(jnp.mean(o * o, axis=-1, keepdims=True) + EPS) * nw
        o_ref[:, h * GDN_DV:(h + 1) * GDN_DV] = (on * _silu(gate[:, h * GDN_DV:(h + 1) * GDN_DV])).astype(o_ref.dtype)

    @pl.when(c == pl.num_programs(1) - 1)
    def _():
        st_ref[...] = s_scr[...]


def _gdn_prompt(proj, nb, seq, w):
    nc = seq // GDN_CHUNK
    q = GDN_CHUNK
    row = lambda n, c: n * nc + c
    const = lambda shape: pl.BlockSpec(shape, lambda n, c: (0, 0))
    return pl.pallas_call(
        _gdn_kernel,
        grid=(nb, nc),
        in_specs=[pl.BlockSpec((q, GDN_CONV), lambda n, c: (row(n, c), OFF_QKV // GDN_CONV)),
                  pl.BlockSpec((q, GDN_V), lambda n, c: (row(n, c), OFF_GATE // GDN_V)),
                  pl.BlockSpec((q, 128), lambda n, c: (row(n, c), OFF_SM // 128)),
                  const((CONV_W, GDN_CONV)), const((1, 128)), const((1, 128)), const((1, GDN_DV))],
        out_specs=[pl.BlockSpec((q, GDN_V), lambda n, c: (row(n, c), 0)),
                   pl.BlockSpec((None, GDN_H, GDN_DK, GDN_DV), lambda n, c: (n, 0, 0, 0))],
        out_shape=[jax.ShapeDtypeStruct((nb * seq, GDN_V), bf16),
                   jax.ShapeDtypeStruct((nb, GDN_H, GDN_DK, GDN_DV), f32)],
        scratch_shapes=[pltpu.VMEM((q + 8, GDN_CONV), f32), pltpu.VMEM((GDN_H, GDN_DK, GDN_DV), f32)],
        compiler_params=_cp(("parallel", "arbitrary")),
        name="gdn_scan",
    )(proj, proj, proj, w["gdn_cw"], w["gdn_dtb"], w["gdn_alog"], w["gdn_nw"])


def _conv_step(buf, xraw, w_ref, width):
    acc = w_ref[CONV_W - 1:CONV_W, :] * xraw
    for k in range(CONV_W - 1):
        acc = acc + w_ref[k:k + 1, :] * buf[:, k * width:(k + 1) * width]
    return acc, jnp.concatenate([buf[:, width:], xraw], axis=1)


def _ssd_step_kernel(z_ref, xs_ref, bc_ref, sm_ref, buf_ref, st_ref, cw_ref, cb_ref, dtb_ref, alog_ref, de_ref,
                     nw_ref, ex_ref, y_ref, nbuf_ref, nst_ref):
    bs = z_ref.shape[0]
    xraw = jnp.concatenate([xs_ref[...], bc_ref[...]], axis=1)
    conv, nbuf = _conv_step(buf_ref[...], xraw, cw_ref, SSD_CONV)
    nbuf_ref[...] = nbuf
    act = _silu(conv + cb_ref[...])
    xs = act[:, :SSD_INNER]
    head_lanes = _lane_mask((bs, 128), SM_DT, SM_DT + SSD_H)
    dt = jnp.where(head_lanes, _softplus(sm_ref[...] + dtb_ref[...]), 0.0)
    da = jnp.exp(dt * (-jnp.exp(alog_ref[...])))
    dt_e = _dot_hi(dt, ex_ref[...])
    da_e = _dot_hi(da, ex_ref[...])
    cols = jnp.concatenate([xs * dt_e, da_e], axis=0).T
    lane = _iota((SSD_INNER, 128), 1)
    ycols = jnp.zeros((SSD_INNER, 128), f32)
    gw = SSD_INNER // SSD_G
    hg = SSD_H // SSD_G
    for b in range(bs):
        xcol = cols[:, b:b + 1]
        dcol = cols[:, bs + b:bs + b + 1]
        ys = []
        for g in range(SSD_G):
            b_row = act[b:b + 1, SSD_INNER + g * SSD_N:SSD_INNER + (g + 1) * SSD_N]
            c_row = act[b:b + 1, SSD_INNER + (SSD_G + g) * SSD_N:SSD_INNER + (SSD_G + g + 1) * SSD_N]
            st = st_ref[b, g * hg:(g + 1) * hg].reshape(gw, SSD_N)
            new = st * dcol[g * gw:(g + 1) * gw] + xcol[g * gw:(g + 1) * gw] * b_row
            nst_ref[b, g * hg:(g + 1) * hg] = new.reshape(hg, SSD_P, SSD_N)
            ys.append(jnp.sum(new * c_row, axis=-1, keepdims=True))
        ycols = jnp.where(lane == b, jnp.concatenate(ys, axis=0), ycols)
    y = ycols.T[0:bs]
    y = (y + de_ref[...] * xs) * _silu(z_ref[...])
    y_ref[...] = _group_rmsnorm(y, nw_ref[...], gw).astype(y_ref.dtype)


def _ssd_step(proj, buf, state, w, bs=8):
    n = proj.shape[0]
    const = lambda shape: pl.BlockSpec(shape, lambda i: (0,) * len(shape))
    return pl.pallas_call(
        _ssd_step_kernel,
        grid=(n // bs,),
        in_specs=[pl.BlockSpec((bs, 1024), lambda i: (i, OFF_Z // 1024)),
                  pl.BlockSpec((bs, 1024), lambda i: (i, OFF_XS // 1024)),
                  pl.BlockSpec((bs, SSD_BC), lambda i: (i, OFF_BC // SSD_BC)),
                  pl.BlockSpec((bs, 128), lambda i: (i, OFF_SM // 128)),
                  pl.BlockSpec((bs, 3 * SSD_CONV), lambda i: (i, 0)),
                  pl.BlockSpec((bs, SSD_H, SSD_P, SSD_N), lambda i: (i, 0, 0, 0)),
                  const((CONV_W, SSD_CONV)), const((1, SSD_CONV)), const((1, 128)), const((1, 128)),
                  const((1, SSD_INNER)), const((1, SSD_INNER)), const((128, SSD_INNER))],
        out_specs=[pl.BlockSpec((bs, SSD_INNER), lambda i: (i, 0)),
                   pl.BlockSpec((bs, 3 * SSD_CONV), lambda i: (i, 0)),
                   pl.BlockSpec((bs, SSD_H, SSD_P, SSD_N), lambda i: (i, 0, 0, 0))],
        out_shape=[jax.ShapeDtypeStruct((n, SSD_INNER), bf16),
                   jax.ShapeDtypeStruct((n, 3 * SSD_CONV), f32),
                   jax.ShapeDtypeStruct((n, SSD_H, SSD_P, SSD_N), f32)],
        compiler_params=_cp(("parallel",)),
        name="ssd_step",
    )(proj, proj, proj, proj, buf, state, w["ssd_cw"], w["ssd_cb"], w["ssd_dtb"], w["ssd_alog"], w["ssd_de"],
      w["ssd_nw"], w["ssd_expand"])


def _gdn_step_kernel(qkv_ref, gate_ref, sm_ref, buf_ref, st_ref, cw_ref, dtb_ref, alog_ref, nw_ref,
                     o_ref, nbuf_ref, nst_ref, oscr):
    bs = qkv_ref.shape[0]
    conv, nbuf = _conv_step(buf_ref[...], qkv_ref[...], cw_ref, GDN_CONV)
    nbuf_ref[...] = nbuf
    act = _silu(conv)
    sm = sm_ref[...]
    eg = jnp.exp(-jnp.exp(alog_ref[...]) * _softplus(sm + dtb_ref[...]))
    beta = jax.nn.sigmoid(sm)
    qs, ks = [], []
    for h in range(GDN_H):
        qh = act[:, h * GDN_DK:(h + 1) * GDN_DK]
        kh = act[:, GDN_H * GDN_DK + h * GDN_DK:GDN_H * GDN_DK + (h + 1) * GDN_DK]
        qs.append(qh * lax.rsqrt(jnp.sum(qh * qh, axis=-1, keepdims=True) + EPS) * (GDN_DK ** -0.5))
        ks.append(kh * lax.rsqrt(jnp.sum(kh * kh, axis=-1, keepdims=True) + EPS))
    qk_t = jnp.concatenate(qs + ks, axis=1).T
    for b in range(bs):
        for h in range(GDN_H):
            qcol = qk_t[h * GDN_DK:(h + 1) * GDN_DK, b:b + 1]
            kcol = qk_t[(GDN_H + h) * GDN_DK:(GDN_H + h + 1) * GDN_DK, b:b + 1]
            vrow = act[b:b + 1, 2 * GDN_H * GDN_DK + h * GDN_DV:2 * GDN_H * GDN_DK + (h + 1) * GDN_DV]
            egs = eg[b:b + 1, SM_A + h:SM_A + h + 1]
            bet = beta[b:b + 1, SM_B + h:SM_B + h + 1]
            s = st_ref[b, h]
            v_new = bet * (vrow - egs * jnp.sum(kcol * s, axis=0, keepdims=True))
            s_new = s * egs + kcol * v_new
            nst_ref[b, h] = s_new
            oscr[b:b + 1, h * GDN_DV:(h + 1) * GDN_DV] = jnp.sum(qcol * s_new, axis=0, keepdims=True)
    o = oscr[...]
    nw = nw_ref[...]
    gate = gate_ref[...]
    for h in range(GDN_H):
        oh = o[:, h * GDN_DV:(h + 1) * GDN_DV]
        on = oh * lax.rsqrt(jnp.mean(oh * oh, axis=-1, keepdims=True) + EPS) * nw
        o_ref[:, h * GDN_DV:(h + 1) * GDN_DV] = (on * _silu(gate[:, h * GDN_DV:(h + 1) * GDN_DV])).astype(o_ref.dtype)


def _gdn_step(proj, buf, state, w, bs=8):
    n = proj.shape[0]
    const = lambda shape: pl.BlockSpec(shape, lambda i: (0,) * len(shape))
    return pl.pallas_call(
        _gdn_step_kernel,
        grid=(n // bs,),
        in_specs=[pl.BlockSpec((bs, GDN_CONV), lambda i: (i, OFF_QKV // GDN_CONV)),
                  pl.BlockSpec((bs, GDN_V), lambda i: (i, OFF_GATE // GDN_V)),
                  pl.BlockSpec((bs, 128), lambda i: (i, OFF_SM // 128)),
                  pl.BlockSpec((bs, 3 * GDN_CONV), lambda i: (i, 0)),
                  pl.BlockSpec((bs, GDN_H, GDN_DK, GDN_DV), lambda i: (i, 0, 0, 0)),
                  const((CONV_W, GDN_CONV)), const((1, 128)), const((1, 128)), const((1, GDN_DV))],
        out_specs=[pl.BlockSpec((bs, GDN_V), lambda i: (i, 0)),
                   pl.BlockSpec((bs, 3 * GDN_CONV), lambda i: (i, 0)),
                   pl.BlockSpec((bs, GDN_H, GDN_DK, GDN_DV), lambda i: (i, 0, 0, 0))],
        out_shape=[jax.ShapeDtypeStruct((n, GDN_V), bf16),
                   jax.ShapeDtypeStruct((n, 3 * GDN_CONV), f32),
                   jax.ShapeDtypeStruct((n, GDN_H, GDN_DK, GDN_DV), f32)],
        scratch_shapes=[pltpu.VMEM((bs, GDN_V), f32)],
        compiler_params=_cp(("parallel",)),
        name="gdn_step",
    )(proj, proj, proj, buf, state, w["gdn_cw"], w["gdn_dtb"], w["gdn_alog"], w["gdn_nw"])


def _outproj_kernel(ys_ref, og_ref, x_ref, g1_ref, sh_ref, sc_ref, nw_ref, w_ref, x1_ref, hf_ref):
    m = _dot(ys_ref[...], w_ref[0:SSD_INNER, :]) + _dot(og_ref[...], w_ref[SSD_INNER:, :])
    x1 = x_ref[...] + g1_ref[...] * m
    x1_ref[...] = x1
    y = x1 * lax.rsqrt(jnp.mean(x1 * x1, axis=-1, keepdims=True) + EPS) * nw_ref[...]
    hf_ref[...] = (y * (1.0 + sc_ref[...]) + sh_ref[...]).astype(hf_ref.dtype)


def _outproj(ys, og, x, mod, norm_w, w_out_b, per_token, rows_per_seq):
    t = x.shape[0]
    tm = min(512, t)
    return pl.pallas_call(
        _outproj_kernel,
        grid=(t // tm,),
        in_specs=[pl.BlockSpec((tm, SSD_INNER), lambda i: (i, 0)),
                  pl.BlockSpec((tm, GDN_V), lambda i: (i, 0)),
                  pl.BlockSpec((tm, D), lambda i: (i, 0)),
                  _mod_specs(per_token, tm, rows_per_seq, 2, 1),
                  _mod_specs(per_token, tm, rows_per_seq, 3, 1),
                  _mod_specs(per_token, tm, rows_per_seq, 4, 1),
                  pl.BlockSpec((1, D), lambda i: (0, 0)),
                  pl.BlockSpec((D, D), lambda i: (0, 0))],
        out_specs=[pl.BlockSpec((tm, D), lambda i: (i, 0)), pl.BlockSpec((tm, D), lambda i: (i, 0))],
        out_shape=[jax.ShapeDtypeStruct((t, D), f32), jax.ShapeDtypeStruct((t, D), bf16)],
        compiler_params=_cp(("parallel",)),
        name="outproj",
    )(ys, og, x, mod, mod, mod, norm_w.reshape(1, D), w_out_b)


def _peer_scores_kernel(hf_ref, wq_ref, sk_ref, o_ref):
    q_t = _dot_nt(wq_ref[...], hf_ref[...])
    half = PEER_DK // 2
    for hs in range(2 * PEER_H):
        o_ref[hs] = _dot(sk_ref[hs], q_t[hs * half:(hs + 1) * half].astype(bf16))


def _peer_scores(hf, wq_t, sk, tm):
    t = hf.shape[0]
    return pl.pallas_call(
        _peer_scores_kernel,
        grid=(t // tm,),
        in_specs=[pl.BlockSpec((tm, D), lambda i: (i, 0)),
                  pl.BlockSpec((PEER_H * PEER_DK, D), lambda i: (0, 0)),
                  pl.BlockSpec((2 * PEER_H, PEER_NK, PEER_DK // 2), lambda i: (0, 0, 0))],
        out_specs=pl.BlockSpec((2 * PEER_H, PEER_NK, tm), lambda i: (0, 0, i)),
        out_shape=jax.ShapeDtypeStruct((2 * PEER_H, PEER_NK, t), f32),
        compiler_params=_cp(("parallel",)),
        name="peer_scores",
    )(hf, wq_t, sk)


_STAIR = [PEER_K // (k1 + 1) for k1 in range(PEER_K)]
_STAIR_ROWS = -(-sum(_STAIR) // 8) * 8


def _peer_select_kernel(s_ref, tau_ref, f0_ref, e1_ref, v1_scr, cand):
    tm = s_ref.shape[-1]
    for h in range(PEER_H):
        s0 = s_ref[2 * h]
        s1 = s_ref[2 * h + 1]
        tops = []
        for side, cur in enumerate((s0, s1)):
            vals = []
            for _ in range(PEER_K):
                m = jnp.max(cur, axis=0, keepdims=True)
                vals.append(m)
                cur = jnp.where(cur == m, NEG, cur)
            tops.append(vals)
        for k in range(PEER_K):
            v1_scr[k:k + 1, :] = tops[1][k]
        cand[_STAIR_ROWS - 8:_STAIR_ROWS, :] = jnp.full((8, tm), NEG, f32)
        off = 0
        for k1 in range(PEER_K):
            cand[off:off + _STAIR[k1], :] = tops[0][k1] + v1_scr[0:_STAIR[k1], :]
            off += _STAIR[k1]
        cv = cand[...]
        cur = cv
        for _ in range(PEER_K):
            tau = jnp.max(cur, axis=0, keepdims=True)
            cur = jnp.where(cur == tau, NEG, cur)
        m0, m1 = tops[0][0], tops[1][0]
        cmax = m0 + m1
        zsum = jnp.sum(jnp.where(cv >= tau, jnp.exp(cv - cmax), 0.0), axis=0, keepdims=True)
        tau_ref[h:h + 1, :] = tau
        f0_ref[h] = jnp.exp(s0 - m0) / zsum
        e1_ref[h] = jnp.exp(s1 - m1)


def _peer_select(scores_t, tm):
    t = scores_t.shape[-1]
    return pl.pallas_call(
        _peer_select_kernel,
        grid=(t // tm,),
        in_specs=[pl.BlockSpec((2 * PEER_H, PEER_NK, tm), lambda i: (0, 0, i))],
        out_specs=[pl.BlockSpec((PEER_H, tm), lambda i: (0, i)),
                   pl.BlockSpec((PEER_H, PEER_NK, tm), lambda i: (0, 0, i)),
                   pl.BlockSpec((PEER_H, PEER_NK, tm), lambda i: (0, 0, i))],
        out_shape=[jax.ShapeDtypeStruct((PEER_H, t), f32),
                   jax.ShapeDtypeStruct((PEER_H, PEER_NK, t), f32),
                   jax.ShapeDtypeStruct((PEER_H, PEER_NK, t), f32)],
        scratch_shapes=[pltpu.VMEM((PEER_K, tm), f32), pltpu.VMEM((_STAIR_ROWS, tm), f32)],
        compiler_params=_cp(("parallel",)),
        name="peer_select",
    )(scores_t)


def _gelu(x):
    return 0.5 * x * (1.0 + lax.erf(x * (1.0 / math.sqrt(2.0))))


def _peer_dense_kernel(hf_ref, u_ref, vt_ref, s0_ref, s1_ref, tau_ref, f0_ref, e1_ref, o_ref, acc, cd):
    j = pl.program_id(1)
    ni1 = u_ref.shape[0] // PEER_NK

    @pl.when(j == 0)
    def _():
        acc[...] = jnp.zeros(acc.shape, f32)

    a_t = _dot_nt(u_ref[...], hf_ref[...])
    base = (j * ni1) % 8
    for i1 in range(ni1):
        wb = None
        for h in range(PEER_H):
            tsum = s0_ref[h, pl.ds(base + i1, 1), :] + s1_ref[h]
            term = jnp.where(tsum >= tau_ref[h:h + 1, :], f0_ref[h, pl.ds(base + i1, 1), :] * e1_ref[h], 0.0)
            wb = term if wb is None else wb + term
        rows = slice(i1 * PEER_NK, (i1 + 1) * PEER_NK)
        cd[rows, :] = (wb * _gelu(a_t[rows, :])).astype(bf16)
    acc[...] += _dot(vt_ref[...], cd[...])

    @pl.when(j == pl.num_programs(1) - 1)
    def _():
        o_ref[...] = acc[...].T


def _peer_dense(hf, u_b, vt_b, scores_t, tau, f0, e1, tm, te):
    t = hf.shape[0]
    ni1 = te // PEER_NK
    s0_view = scores_t.reshape(PEER_H, 2 * PEER_NK, t)
    return pl.pallas_call(
        _peer_dense_kernel,
        grid=(t // tm, PEER_E // te),
        in_specs=[pl.BlockSpec((tm, D), lambda i, j: (i, 0)),
                  pl.BlockSpec((te, D), lambda i, j: (j, 0)),
                  pl.BlockSpec((D, te), lambda i, j: (0, j)),
                  pl.BlockSpec((PEER_H, 8, tm), lambda i, j: (0, (j * ni1) // 8, i)),
                  pl.BlockSpec((PEER_H, PEER_NK, tm), lambda i, j: (0, 1, i)),
                  pl.BlockSpec((PEER_H, tm), lambda i, j: (0, i)),
                  pl.BlockSpec((PEER_H, 8, tm), lambda i, j: (0, (j * ni1) // 8, i)),
                  pl.BlockSpec((PEER_H, PEER_NK, tm), lambda i, j: (0, 0, i))],
        out_specs=pl.BlockSpec((tm, D), lambda i, j: (i, 0)),
        out_shape=jax.ShapeDtypeStruct((t, D), f32),
        scratch_shapes=[pltpu.VMEM((D, tm), f32), pltpu.VMEM((te, tm), bf16)],
        compiler_params=_cp(("parallel", "arbitrary")),
        name="peer_dense",
    )(hf, u_b, vt_b, s0_view, s0_view, tau, f0, e1)


def _final_kernel(x1_ref, p_ref, g2_ref, nw_ref, o_ref):
    x2 = x1_ref[...] + g2_ref[...] * p_ref[...]
    o_ref[...] = x2 * lax.rsqrt(jnp.mean(x2 * x2, axis=-1, keepdims=True) + EPS) * nw_ref[...]


def _final(x1, peer_out, mod, norm_w, per_token, rows_per_seq, row_off):
    t = x1.shape[0]
    tm = min(512, t)
    off = row_off // tm
    return pl.pallas_call(
        _final_kernel,
        grid=(t // tm,),
        in_specs=[pl.BlockSpec((tm, D), lambda i: (i, 0)),
                  pl.BlockSpec((tm, D), lambda i: (i + off, 0)),
                  _mod_specs(per_token, tm, rows_per_seq, 5, 1),
                  pl.BlockSpec((1, D), lambda i: (0, 0))],
        out_specs=pl.BlockSpec((tm, D), lambda i: (i, 0)),
        out_shape=jax.ShapeDtypeStruct((t, D), f32),
        compiler_params=_cp(("parallel",)),
        name="final_norm",
    )(x1, peer_out, mod, norm_w.reshape(1, D))


def _pad_row(v, lo):
    return jnp.zeros((1, 128), f32).at[0, lo:lo + v.shape[0]].set(v.astype(f32))


def _prep_weights(w_in, ssd_conv_w, ssd_conv_b, ssd_dt_bias, ssd_A_log, ssd_D, ssd_norm_w, gdn_conv_w,
                  gdn_dt_bias, gdn_A_log, gdn_norm_w):
    o_z, o_xbc = 0, SSD_INNER
    o_dt = o_xbc + SSD_CONV
    o_qkv = o_dt + SSD_H
    o_a = o_qkv + GDN_CONV
    o_b = o_a + GDN_H
    o_gate = o_b + GDN_H
    cols = [w_in[:, o_z:o_z + SSD_INNER], w_in[:, o_xbc:o_xbc + SSD_INNER], w_in[:, o_gate:o_gate + GDN_V],
            w_in[:, o_qkv:o_qkv + GDN_CONV], w_in[:, o_xbc + SSD_INNER:o_xbc + SSD_CONV],
            w_in[:, o_dt:o_dt + SSD_H], w_in[:, o_a:o_a + GDN_H], w_in[:, o_b:o_b + GDN_H]]
    w_re = jnp.concatenate(cols, axis=1)
    w_re = jnp.pad(w_re, ((0, 0), (0, P_COLS - w_re.shape[1]))).astype(bf16)
    expand = (jnp.arange(128)[:, None] == (jnp.arange(SSD_INNER)[None, :] // SSD_P)).astype(f32)
    return dict(
        w_in_re=w_re,
        ssd_wxs=ssd_conv_w[:, :SSD_INNER], ssd_wbc=ssd_conv_w[:, SSD_INNER:],
        ssd_bxs=ssd_conv_b[None, :SSD_INNER], ssd_bbc=ssd_conv_b[None, SSD_INNER:],
        ssd_cw=ssd_conv_w, ssd_cb=ssd_conv_b[None, :],
        ssd_dtb=_pad_row(ssd_dt_bias, SM_DT), ssd_alog=_pad_row(ssd_A_log, SM_DT),
        ssd_de=jnp.repeat(ssd_D.astype(f32), SSD_P)[None, :], ssd_nw=ssd_norm_w[None, :].astype(f32),
        ssd_expand=expand,
        gdn_cw=gdn_conv_w, gdn_dtb=_pad_row(gdn_dt_bias, SM_A), gdn_alog=_pad_row(gdn_A_log, SM_A),
        gdn_nw=gdn_norm_w[None, :].astype(f32),
    )


def _peer_tile(t):
    for tm in (640, 512, 384, 256, 128):
        if t % tm == 0:
            return tm
    raise ValueError(f"token count {t} must be a multiple of 128")


def kernel(x_prompt, x_sample, c_prompt, c_sample, state_ssd, state_ssd_conv, state_gdn, state_gdn_conv, w_ada, b_ada, norm1_w, norm2_w, w_in, ssd_conv_w, ssd_conv_b, ssd_dt_bias, ssd_A_log, ssd_D, ssd_norm_w, gdn_conv_w, gdn_dt_bias, gdn_A_log, gdn_norm_w, w_out, peer_w_q, peer_sub_keys, peer_u, peer_v, final_norm_w):
    assert w_ada.shape[0] == 1, "single layer"
    nb, seq, _ = x_prompt.shape
    ns = x_sample.shape[0]
    assert x_sample.shape[1] == 1 and seq % SSD_CHUNK == 0 and seq % 512 == 0 and ns % 8 == 0
    w = _prep_weights(w_in[0], ssd_conv_w[0], ssd_conv_b[0], ssd_dt_bias[0], ssd_A_log[0], ssd_D[0], ssd_norm_w[0],
                      gdn_conv_w[0], gdn_dt_bias[0], gdn_A_log[0], gdn_norm_w[0])
    w_out_b = w_out[0].astype(bf16)
    wq_t = peer_w_q[0].T.astype(bf16)
    sk = peer_sub_keys[0].reshape(2 * PEER_H, PEER_NK, PEER_DK // 2).astype(bf16)
    u_b = peer_u[0].astype(bf16)
    vt_b = peer_v[0].T.astype(bf16)

    nbp = -(-nb // 8) * 8
    c_all = jnp.concatenate([jnp.pad(c_prompt, ((0, nbp - nb), (0, 0))), c_sample], axis=0)
    mod = _modulation(c_all, w_ada[0], b_ada[0])
    mod_p = mod[:nb].reshape(nb, 1, 6 * D)
    mod_s = mod[nbp:]

    xp = x_prompt.reshape(nb * seq, D)
    xs = x_sample.reshape(ns, D)

    proj_p = _inproj(xp, mod_p, norm1_w[0], w["w_in_re"], False, seq)
    y_ssd_p, ssd_state_p = _ssd_prompt(proj_p, nb, seq, w)
    o_gdn_p, gdn_state_p = _gdn_prompt(proj_p, nb, seq, w)
    x1_p, hf_p = _outproj(y_ssd_p, o_gdn_p, xp, mod_p, norm2_w[0], w_out_b, False, seq)
    tail = proj_p.reshape(nb, seq, P_COLS)[:, seq - (CONV_W - 1):, :]
    ssd_conv_p = jnp.concatenate([tail[..., OFF_XS:OFF_XS + SSD_INNER], tail[..., OFF_BC:OFF_BC + SSD_BC]], axis=-1)
    gdn_conv_p = tail[..., OFF_QKV:OFF_QKV + GDN_CONV]

    proj_s = _inproj(xs, mod_s, norm1_w[0], w["w_in_re"], True, 1)
    y_ssd_s, ssd_conv_s, ssd_state_s = _ssd_step(proj_s, state_ssd_conv[0].reshape(ns, 3 * SSD_CONV), state_ssd[0], w)
    o_gdn_s, gdn_conv_s, gdn_state_s = _gdn_step(proj_s, state_gdn_conv[0].reshape(ns, 3 * GDN_CONV), state_gdn[0], w)
    x1_s, hf_s = _outproj(y_ssd_s, o_gdn_s, xs, mod_s, norm2_w[0], w_out_b, True, 1)

    hf = jnp.concatenate([hf_p, hf_s], axis=0)
    tm = _peer_tile(hf.shape[0])
    scores_t = _peer_scores(hf, wq_t, sk, tm)
    tau, f0, e1 = _peer_select(scores_t, tm)
    peer_out = _peer_dense(hf, u_b, vt_b, scores_t, tau, f0, e1, tm, 512)

    y_p = _final(x1_p, peer_out, mod_p, final_norm_w, False, seq, 0)
    y_s = _final(x1_s, peer_out, mod_s, final_norm_w, True, 1, nb * seq)

    return (y_p.reshape(nb, seq, D), y_s.reshape(ns, 1, D),
            ssd_state_p[None], ssd_conv_p[None], gdn_state_p[None], gdn_conv_p[None],
            ssd_state_s[None], ssd_conv_s.reshape(1, ns, CONV_W - 1, SSD_CONV),
            gdn_state_s[None], gdn_conv_s.reshape(1, ns, CONV_W - 1, GDN_CONV))
```

```python
import functools
import math

import jax
import jax.numpy as jnp
from jax import lax
from jax.experimental import pallas as pl
from jax.experimental.pallas import tpu as pltpu

f32 = jnp.float32
bf16 = jnp.bfloat16

D = 2048
CONV_W = 4
SSD_P = 64
SSD_INNER = D // 2
SSD_H = SSD_INNER // SSD_P
SSD_G = 2
SSD_N = 128
SSD_CHUNK = 128
SSD_BC = 2 * SSD_G * SSD_N
SSD_CONV = SSD_INNER + SSD_BC
GDN_DK = 128
GDN_DV = 128
GDN_V = D // 2
GDN_H = GDN_V // GDN_DV
GDN_CHUNK = 64
GDN_CONV = GDN_H * (2 * GDN_DK + GDN_DV)
PEER_H = 8
PEER_NK = 128
PEER_E = PEER_NK * PEER_NK
PEER_DK = 256
PEER_K = 16
EPS = 1e-6
NEG = -1e30

OFF_Z, OFF_XS, OFF_GATE, OFF_QKV, OFF_BC, OFF_SM = 0, 1024, 2048, 3072, 6144, 6656
P_COLS = 7168
SM_DT, SM_A, SM_B = 0, 16, 24

VMEM_LIMIT = 56 * 1024 * 1024


def _cp(sem):
    return pltpu.CompilerParams(dimension_semantics=sem, vmem_limit_bytes=VMEM_LIMIT)


def _silu(x):
    return x * jax.nn.sigmoid(x)


def _softplus(x):
    return jnp.maximum(x, 0.0) + jnp.log1p(jnp.exp(-jnp.abs(x)))


def _dot(a, b):
    return jnp.dot(a, b, preferred_element_type=f32)


def _dot_nt(a, b):
    return lax.dot_general(a, b, (((1,), (1,)), ((), ())), preferred_element_type=f32)


def _dot_hi(a, b):
    return jnp.dot(a, b, preferred_element_type=f32, precision=lax.Precision.HIGHEST)


def _iota(shape, dim):
    return lax.broadcasted_iota(jnp.int32, shape, dim)


def _mod_kernel(c_ref, w_ref, b_ref, o_ref):
    a = _silu(c_ref[...]).astype(bf16)
    o_ref[...] = _dot(a, w_ref[...].astype(bf16)) + b_ref[...]


def _modulation(c_all, w_ada, b_ada):
    m, tn = c_all.shape[0], 1024
    return pl.pallas_call(
        _mod_kernel,
        grid=(6 * D // tn,),
        in_specs=[pl.BlockSpec((m, D), lambda j: (0, 0)),
                  pl.BlockSpec((D, tn), lambda j: (0, j)),
                  pl.BlockSpec((1, tn), lambda j: (0, j))],
        out_specs=pl.BlockSpec((m, tn), lambda j: (0, j)),
        out_shape=jax.ShapeDtypeStruct((m, 6 * D), f32),
        compiler_params=_cp(("arbitrary",)),
        name="adaln_mod",
    )(c_all, w_ada, b_ada.reshape(1, 6 * D))


def _mod_specs(per_token, tm, rows_per_seq, which, grid_rank):
    if per_token:
        if grid_rank == 2:
            return pl.BlockSpec((tm, D), lambda i, j: (i, which))
        return pl.BlockSpec((tm, D), lambda i: (i, which))
    tps = rows_per_seq // tm
    if grid_rank == 2:
        return pl.BlockSpec((None, 1, D), lambda i, j: (i // tps, 0, which))
    return pl.BlockSpec((None, 1, D), lambda i: (i // tps, 0, which))


def _inproj_kernel(x_ref, sh_ref, sc_ref, nw_ref, w_ref, o_ref, hm_ref):
    @pl.when(pl.program_id(1) == 0)
    def _():
        x = x_ref[...]
        y = x * lax.rsqrt(jnp.mean(x * x, axis=-1, keepdims=True) + EPS) * nw_ref[...]
        hm_ref[...] = (y * (1.0 + sc_ref[...]) + sh_ref[...]).astype(bf16)

    o_ref[...] = _dot(hm_ref[...], w_ref[...])


def _inproj(x, mod, norm_w, w_re, per_token, rows_per_seq):
    t = x.shape[0]
    tm = min(512, t)
    tn = 1024
    return pl.pallas_call(
        _inproj_kernel,
        grid=(t // tm, P_COLS // tn),
        in_specs=[pl.BlockSpec((tm, D), lambda i, j: (i, 0)),
                  _mod_specs(per_token, tm, rows_per_seq, 0, 2),
                  _mod_specs(per_token, tm, rows_per_seq, 1, 2),
                  pl.BlockSpec((1, D), lambda i, j: (0, 0)),
                  pl.BlockSpec((D, tn), lambda i, j: (0, j))],
        out_specs=pl.BlockSpec((tm, tn), lambda i, j: (i, j)),
        out_shape=jax.ShapeDtypeStruct((t, P_COLS), f32),
        scratch_shapes=[pltpu.VMEM((tm, D), bf16)],
        compiler_params=_cp(("parallel", "arbitrary")),
        name="inproj",
    )(x, mod, mod, norm_w.reshape(1, D), w_re)


def _conv_chunk(ext_ref, cur, w_ref, b_row, q):
    ext_ref[8:8 + q, :] = cur
    acc = w_ref[CONV_W - 1:CONV_W, :] * cur
    for k in range(CONV_W - 1):
        acc = acc + w_ref[k:k + 1, :] * ext_ref[5 + k:5 + k + q, :]
    ext_ref[0:8, :] = cur[q - 8:q, :]
    if b_row is not None:
        acc = acc + b_row
    return acc


def _group_rmsnorm(y, nw, width):
    outs = []
    for g in range(y.shape[-1] // width):
        seg = y[:, g * width:(g + 1) * width]
        ms = jnp.mean(seg * seg, axis=-1, keepdims=True)
        outs.append(seg * lax.rsqrt(ms + EPS) * nw[:, g * width:(g + 1) * width])
    return jnp.concatenate(outs, axis=-1)


def _lane_mask(shape, lo, hi):
    lane = _iota(shape, len(shape) - 1)
    return (lane >= lo) & (lane < hi)


def _ssd_kernel(z_ref, xs_ref, bc_ref, sm_ref, wxs_ref, wbc_ref, bxs_ref, bbc_ref, dtb_ref, alog_ref, de_ref,
                nw_ref, y_ref, st_ref, extx, extb, ht, yscr):
    q = SSD_CHUNK
    c = pl.program_id(1)

    @pl.when(c == 0)
    def _():
        extx[0:8, :] = jnp.zeros((8, SSD_INNER), f32)
        extb[0:8, :] = jnp.zeros((8, SSD_BC), f32)
        ht[...] = jnp.zeros(ht.shape, f32)

    xs = _silu(_conv_chunk(extx, xs_ref[...], wxs_ref, bxs_ref[...], q))
    bcv = _silu(_conv_chunk(extb, bc_ref[...], wbc_ref, bbc_ref[...], q))

    head_lanes = _lane_mask((q, 128), SM_DT, SM_DT + SSD_H)
    dt = jnp.where(head_lanes, _softplus(sm_ref[...] + dtb_ref[...]), 0.0)
    a_row = -jnp.exp(alog_ref[...])
    tri = (_iota((q, q), 0) >= _iota((q, q), 1))
    cs = _dot_hi(tri.astype(f32), dt * a_row)
    cs_t = cs.T
    dt_t = dt.T
    last_col = cs_t[:, q - 1:q]
    w_t = dt_t * jnp.exp(last_col - cs_t)
    ecs = jnp.exp(cs)
    lane_lo = _iota((q, 128), 1) < SSD_P

    for g in range(SSD_G):
        b_g = bcv[:, g * SSD_N:(g + 1) * SSD_N]
        c_g = bcv[:, SSD_G * SSD_N + g * SSD_N:SSD_G * SSD_N + (g + 1) * SSD_N]
        b_gt = b_g.T
        cb = _dot(c_g.astype(bf16), b_gt.astype(bf16))
        for r2 in range(SSD_H // SSD_G // 2):
            pair = g * (SSD_H // SSD_G // 2) + r2
            xs_pair = xs[:, pair * 128:(pair + 1) * 128]
            xs_pair_b = xs_pair.astype(bf16)
            h_prev = ht[pair]
            rhs = jnp.concatenate([xs_pair_b, h_prev.astype(bf16)], axis=0)
            ys, hs, els = [], [], []
            for e in range(2):
                h = 2 * pair + e
                diff = cs[:, h:h + 1] - cs_t[h:h + 1, :]
                dec = jnp.exp(jnp.where(tri, diff, NEG))
                m = cb * dec * dt_t[h:h + 1, :]
                lhs = jnp.concatenate([m, c_g * ecs[:, h:h + 1]], axis=1).astype(bf16)
                ys.append(_dot(lhs, rhs))
                bw = (b_gt * w_t[h:h + 1, :]).astype(bf16)
                hs.append(_dot(bw, xs_pair_b))
                els.append(jnp.exp(last_col[h:h + 1, :]))
            yscr[:, pair * 128:(pair + 1) * 128] = jnp.where(lane_lo, ys[0], ys[1])
            ht[pair] = h_prev * jnp.where(lane_lo, els[0], els[1]) + jnp.where(lane_lo, hs[0], hs[1])

    z = z_ref[...]
    y = (yscr[...] + de_ref[...] * xs) * _silu(z)
    y_ref[...] = _group_rmsnorm(y, nw_ref[...], SSD_INNER // SSD_G).astype(y_ref.dtype)

    @pl.when(c == pl.num_programs(1) - 1)
    def _():
        for pair in range(SSD_H // 2):
            t = ht[pair].T
            st_ref[2 * pair] = t[0:SSD_P]
            st_ref[2 * pair + 1] = t[SSD_P:2 * SSD_P]


def _ssd_prompt(proj, nb, seq, w):
    nc = seq // SSD_CHUNK
    q = SSD_CHUNK
    row = lambda n, c: n * nc + c
    const = lambda shape: pl.BlockSpec(shape, lambda n, c: (0, 0))
    return pl.pallas_call(
        _ssd_kernel,
        grid=(nb, nc),
        in_specs=[pl.BlockSpec((q, 1024), lambda n, c: (row(n, c), OFF_Z // 1024)),
                  pl.BlockSpec((q, 1024), lambda n, c: (row(n, c), OFF_XS // 1024)),
                  pl.BlockSpec((q, SSD_BC), lambda n, c: (row(n, c), OFF_BC // SSD_BC)),
                  pl.BlockSpec((q, 128), lambda n, c: (row(n, c), OFF_SM // 128)),
                  const((CONV_W, SSD_INNER)), const((CONV_W, SSD_BC)), const((1, SSD_INNER)), const((1, SSD_BC)),
                  const((1, 128)), const((1, 128)), const((1, SSD_INNER)), const((1, SSD_INNER))],
        out_specs=[pl.BlockSpec((q, SSD_INNER), lambda n, c: (row(n, c), 0)),
                   pl.BlockSpec((None, SSD_H, SSD_P, SSD_N), lambda n, c: (n, 0, 0, 0))],
        out_shape=[jax.ShapeDtypeStruct((nb * seq, SSD_INNER), bf16),
                   jax.ShapeDtypeStruct((nb, SSD_H, SSD_P, SSD_N), f32)],
        scratch_shapes=[pltpu.VMEM((q + 8, SSD_INNER), f32), pltpu.VMEM((q + 8, SSD_BC), f32),
                        pltpu.VMEM((SSD_H // 2, SSD_N, 2 * SSD_P), f32), pltpu.VMEM((q, SSD_INNER), f32)],
        compiler_params=_cp(("parallel", "arbitrary")),
        name="ssd_scan",
    )(proj, proj, proj, proj, w["ssd_wxs"], w["ssd_wbc"], w["ssd_bxs"], w["ssd_bbc"], w["ssd_dtb"],
      w["ssd_alog"], w["ssd_de"], w["ssd_nw"])


def _gdn_kernel(qkv_ref, gate_ref, sm_ref, cw_ref, dtb_ref, alog_ref, nw_ref, o_ref, st_ref, ext, s_scr):
    q = GDN_CHUNK
    c = pl.program_id(1)

    @pl.when(c == 0)
    def _():
        ext[0:8, :] = jnp.zeros((8, GDN_CONV), f32)
        s_scr[...] = jnp.zeros(s_scr.shape, f32)

    act = _silu(_conv_chunk(ext, qkv_ref[...], cw_ref, None, q))
    sm = sm_ref[...]
    a_lanes = _lane_mask((q, 128), SM_A, SM_A + GDN_H)
    g = jnp.where(a_lanes, -jnp.exp(alog_ref[...]) * _softplus(sm + dtb_ref[...]), 0.0)
    beta = jax.nn.sigmoid(sm)
    incl = _iota((q, q), 0) >= _iota((q, q), 1)
    strict = _iota((q, q), 0) > _iota((q, q), 1)
    gc = _dot_hi(incl.astype(f32), g)
    gc_t = gc.T
    egc = jnp.exp(gc)
    gate = gate_ref[...]
    nw = nw_ref[...]

    heads = range(GDN_H)
    rb, cb_ = _iota((q, q), 0), _iota((q, q), 1)
    same16 = (rb // 16) == (cb_ // 16)
    same32 = (rb // 32) == (cb_ // 32)
    diag_mask = strict & same16
    off32_mask = strict & same32 & jnp.logical_not(same16)
    off64_mask = strict & jnp.logical_not(same32)

    qn, kn, kb, dec, ecol, bcol, gcol, vh = [], [], [], [], [], [], [], []
    for h in heads:
        qh = act[:, h * GDN_DK:(h + 1) * GDN_DK]
        kh = act[:, GDN_H * GDN_DK + h * GDN_DK:GDN_H * GDN_DK + (h + 1) * GDN_DK]
        vh.append(act[:, 2 * GDN_H * GDN_DK + h * GDN_DV:2 * GDN_H * GDN_DK + (h + 1) * GDN_DV])
        qn.append(qh * lax.rsqrt(jnp.sum(qh * qh, axis=-1, keepdims=True) + EPS) * (GDN_DK ** -0.5))
        kn.append(kh * lax.rsqrt(jnp.sum(kh * kh, axis=-1, keepdims=True) + EPS))
        gcol.append(gc[:, SM_A + h:SM_A + h + 1])
        grow = gc_t[SM_A + h:SM_A + h + 1, :]
        ecol.append(egc[:, SM_A + h:SM_A + h + 1])
        bcol.append(beta[:, SM_B + h:SM_B + h + 1])
        dec.append(jnp.exp(jnp.where(incl, gcol[h] - grow, NEG)))
        kb.append(kn[h] * bcol[h])
    kn_b = [kn[h].astype(bf16) for h in heads]
    lmat = [_dot_nt(kb[h].astype(bf16), kn_b[h]) * dec[h] for h in heads]
    qk = [_dot_nt(qn[h].astype(bf16), kn_b[h]) * dec[h] for h in heads]

    dotb = lambda a, b: _dot(a.astype(bf16), b.astype(bf16))
    n_pow = [jnp.where(diag_mask, -lmat[h], 0.0) for h in heads]
    t_m = list(n_pow)
    for _ in range(3):
        n_pow = [dotb(n_pow[h], n_pow[h]) for h in heads]
        t_m = [t_m[h] + n_pow[h] + dotb(t_m[h], n_pow[h]) for h in heads]
    for mask in (off32_mask, off64_mask):
        off = [jnp.where(mask, lmat[h], 0.0) for h in heads]
        y = [off[h] + dotb(off[h], t_m[h]) for h in heads]
        t_m = [t_m[h] - (y[h] + dotb(t_m[h], y[h])) for h in heads]

    rhs = [jnp.concatenate([vh[h] * bcol[h], kb[h] * ecol[h]], axis=1) for h in heads]
    sol = [rhs[h] + dotb(t_m[h], rhs[h]) for h in heads]
    s_prev = [s_scr[h] for h in heads]
    s_b = [s_prev[h].astype(bf16) for h in heads]
    v_new = [sol[h][:, :GDN_DV] - _dot(sol[h][:, GDN_DV:].astype(bf16), s_b[h]) for h in heads]
    v_new_b = [v_new[h].astype(bf16) for h in heads]
    o = [_dot((qn[h] * ecol[h]).astype(bf16), s_b[h]) + _dot(qk[h].astype(bf16), v_new_b[h]) for h in heads]
    for h in heads:
        glast = gcol[h][q - 1:q, :]
        kdec = kn[h] * jnp.exp(glast - gcol[h])
        s_scr[h] = s_prev[h] * jnp.exp(glast) + _dot(kdec.T.astype(bf16), v_new_b[h])
    for h in heads:
        on = o[h] * lax.rsqrt(jnp.mean(o[h] * o[h], axis=-1, keepdims=True) + EPS) * nw
        o_ref[:, h * GDN_DV:(h + 1) * GDN_DV] = (on * _silu(gate[:, h * GDN_DV:(h + 1) * GDN_DV])).astype(o_ref.dtype)

    @pl.when(c == pl.num_programs(1) - 1)
    def _():
        st_ref[...] = s_scr[...]


def _gdn_prompt(proj, nb, seq, w):
    nc = seq // GDN_CHUNK
    q = GDN_CHUNK
    row = lambda n, c: n * nc + c
    const = lambda shape: pl.BlockSpec(shape, lambda n, c: (0, 0))
    return pl.pallas_call(
        _gdn_kernel,
        grid=(nb, nc),
        in_specs=[pl.BlockSpec((q, GDN_CONV), lambda n, c: (row(n, c), OFF_QKV // GDN_CONV)),
                  pl.BlockSpec((q, GDN_V), lambda n, c: (row(n, c), OFF_GATE // GDN_V)),
                  pl.BlockSpec((q, 128), lambda n, c: (row(n, c), OFF_SM // 128)),
                  const((CONV_W, GDN_CONV)), const((1, 128)), const((1, 128)), const((1, GDN_DV))],
        out_specs=[pl.BlockSpec((q, GDN_V), lambda n, c: (row(n, c), 0)),
                   pl.BlockSpec((None, GDN_H, GDN_DK, GDN_DV), lambda n, c: (n, 0, 0, 0))],
        out_shape=[jax.ShapeDtypeStruct((nb * seq, GDN_V), bf16),
                   jax.ShapeDtypeStruct((nb, GDN_H, GDN_DK, GDN_DV), f32)],
        scratch_shapes=[pltpu.VMEM((q + 8, GDN_CONV), f32), pltpu.VMEM((GDN_H, GDN_DK, GDN_DV), f32)],
        compiler_params=_cp(("parallel", "arbitrary")),
        name="gdn_scan",
    )(proj, proj, proj, w["gdn_cw"], w["gdn_dtb"], w["gdn_alog"], w["gdn_nw"])


def _conv_step(buf, xraw, w_ref, width):
    acc = w_ref[CONV_W - 1:CONV_W, :] * xraw
    for k in range(CONV_W - 1):
        acc = acc + w_ref[k:k + 1, :] * buf[:, k * width:(k + 1) * width]
    return acc, jnp.concatenate([buf[:, width:], xraw], axis=1)


def _ssd_step_kernel(z_ref, xs_ref, bc_ref, sm_ref, buf_ref, st_ref, cw_ref, cb_ref, dtb_ref, alog_ref, de_ref,
                     nw_ref, ex_ref, y_ref, nbuf_ref, nst_ref):
    bs = z_ref.shape[0]
    xraw = jnp.concatenate([xs_ref[...], bc_ref[...]], axis=1)
    conv, nbuf = _conv_step(buf_ref[...], xraw, cw_ref, SSD_CONV)
    nbuf_ref[...] = nbuf
    act = _silu(conv + cb_ref[...])
    xs = act[:, :SSD_INNER]
    head_lanes = _lane_mask((bs, 128), SM_DT, SM_DT + SSD_H)
    dt = jnp.where(head_lanes, _softplus(sm_ref[...] + dtb_ref[...]), 0.0)
    da = jnp.exp(dt * (-jnp.exp(alog_ref[...])))
    dt_e = _dot_hi(dt, ex_ref[...])
    da_e = _dot_hi(da, ex_ref[...])
    cols = jnp.concatenate([xs * dt_e, da_e], axis=0).T
    lane = _iota((SSD_INNER, 128), 1)
    ycols = jnp.zeros((SSD_INNER, 128), f32)
    gw = SSD_INNER // SSD_G
    hg = SSD_H // SSD_G
    for b in range(bs):
        xcol = cols[:, b:b + 1]
        dcol = cols[:, bs + b:bs + b + 1]
        ys = []
        for g in range(SSD_G):
            b_row = act[b:b + 1, SSD_INNER + g * SSD_N:SSD_INNER + (g + 1) * SSD_N]
            c_row = act[b:b + 1, SSD_INNER + (SSD_G + g) * SSD_N:SSD_INNER + (SSD_G + g + 1) * SSD_N]
            st = st_ref[b, g * hg:(g + 1) * hg].reshape(gw, SSD_N)
            new = st * dcol[g * gw:(g + 1) * gw] + xcol[g * gw:(g + 1) * gw] * b_row
            nst_ref[b, g * hg:(g + 1) * hg] = new.reshape(hg, SSD_P, SSD_N)
            ys.append(jnp.sum(new * c_row, axis=-1, keepdims=True))
        ycols = jnp.where(lane == b, jnp.concatenate(ys, axis=0), ycols)
    y = ycols.T[0:bs]
    y = (y + de_ref[...] * xs) * _silu(z_ref[...])
    y_ref[...] = _group_rmsnorm(y, nw_ref[...], gw).astype(y_ref.dtype)


def _ssd_step(proj, buf, state, w, bs=8):
    n = proj.shape[0]
    const = lambda shape: pl.BlockSpec(shape, lambda i: (0,) * len(shape))
    return pl.pallas_call(
        _ssd_step_kernel,
        grid=(n // bs,),
        in_specs=[pl.BlockSpec((bs, 1024), lambda i: (i, OFF_Z // 1024)),
                  pl.BlockSpec((bs, 1024), lambda i: (i, OFF_XS // 1024)),
                  pl.BlockSpec((bs, SSD_BC), lambda i: (i, OFF_BC // SSD_BC)),
                  pl.BlockSpec((bs, 128), lambda i: (i, OFF_SM // 128)),
                  pl.BlockSpec((bs, 3 * SSD_CONV), lambda i: (i, 0)),
                  pl.BlockSpec((bs, SSD_H, SSD_P, SSD_N), lambda i: (i, 0, 0, 0)),
                  const((CONV_W, SSD_CONV)), const((1, SSD_CONV)), const((1, 128)), const((1, 128)),
                  const((1, SSD_INNER)), const((1, SSD_INNER)), const((128, SSD_INNER))],
        out_specs=[pl.BlockSpec((bs, SSD_INNER), lambda i: (i, 0)),
                   pl.BlockSpec((bs, 3 * SSD_CONV), lambda i: (i, 0)),
                   pl.BlockSpec((bs, SSD_H, SSD_P, SSD_N), lambda i: (i, 0, 0, 0))],
        out_shape=[jax.ShapeDtypeStruct((n, SSD_INNER), bf16),
                   jax.ShapeDtypeStruct((n, 3 * SSD_CONV), f32),
                   jax.ShapeDtypeStruct((n, SSD_H, SSD_P, SSD_N), f32)],
        compiler_params=_cp(("parallel",)),
        name="ssd_step",
    )(proj, proj, proj, proj, buf, state, w["ssd_cw"], w["ssd_cb"], w["ssd_dtb"], w["ssd_alog"], w["ssd_de"],
      w["ssd_nw"], w["ssd_expand"])


def _gdn_step_kernel(qkv_ref, gate_ref, sm_ref, buf_ref, st_ref, cw_ref, dtb_ref, alog_ref, nw_ref,
                     o_ref, nbuf_ref, nst_ref, oscr):
    bs = qkv_ref.shape[0]
    conv, nbuf = _conv_step(buf_ref[...], qkv_ref[...], cw_ref, GDN_CONV)
    nbuf_ref[...] = nbuf
    act = _silu(conv)
    sm = sm_ref[...]
    eg = jnp.exp(-jnp.exp(alog_ref[...]) * _softplus(sm + dtb_ref[...]))
    beta = jax.nn.sigmoid(sm)
    qs, ks = [], []
    for h in range(GDN_H):
        qh = act[:, h * GDN_DK:(h + 1) * GDN_DK]
        kh = act[:, GDN_H * GDN_DK + h * GDN_DK:GDN_H * GDN_DK + (h + 1) * GDN_DK]
        qs.append(qh * lax.rsqrt(jnp.sum(qh * qh, axis=-1, keepdims=True) + EPS) * (GDN_DK ** -0.5))
        ks.append(kh * lax.rsqrt(jnp.sum(kh * kh, axis=-1, keepdims=True) + EPS))
    qk_t = jnp.concatenate(qs + ks, axis=1).T
    for b in range(bs):
        for h in range(GDN_H):
            qcol = qk_t[h * GDN_DK:(h + 1) * GDN_DK, b:b + 1]
            kcol = qk_t[(GDN_H + h) * GDN_DK:(GDN_H + h + 1) * GDN_DK, b:b + 1]
            vrow = act[b:b + 1, 2 * GDN_H * GDN_DK + h * GDN_DV:2 * GDN_H * GDN_DK + (h + 1) * GDN_DV]
            egs = eg[b:b + 1, SM_A + h:SM_A + h + 1]
            bet = beta[b:b + 1, SM_B + h:SM_B + h + 1]
            s = st_ref[b, h]
            v_new = bet * (vrow - egs * jnp.sum(kcol * s, axis=0, keepdims=True))
            s_new = s * egs + kcol * v_new
            nst_ref[b, h] = s_new
            oscr[b:b + 1, h * GDN_DV:(h + 1) * GDN_DV] = jnp.sum(qcol * s_new, axis=0, keepdims=True)
    o = oscr[...]
    nw = nw_ref[...]
    gate = gate_ref[...]
    for h in range(GDN_H):
        oh = o[:, h * GDN_DV:(h + 1) * GDN_DV]
        on = oh * lax.rsqrt(jnp.mean(oh * oh, axis=-1, keepdims=True) + EPS) * nw
        o_ref[:, h * GDN_DV:(h + 1) * GDN_DV] = (on * _silu(gate[:, h * GDN_DV:(h + 1) * GDN_DV])).astype(o_ref.dtype)


def _gdn_step(proj, buf, state, w, bs=8):
    n = proj.shape[0]
    const = lambda shape: pl.BlockSpec(shape, lambda i: (0,) * len(shape))
    return pl.pallas_call(
        _gdn_step_kernel,
        grid=(n // bs,),
        in_specs=[pl.BlockSpec((bs, GDN_CONV), lambda i: (i, OFF_QKV // GDN_CONV)),
                  pl.BlockSpec((bs, GDN_V), lambda i: (i, OFF_GATE // GDN_V)),
                  pl.BlockSpec((bs, 128), lambda i: (i, OFF_SM // 128)),
                  pl.BlockSpec((bs, 3 * GDN_CONV), lambda i: (i, 0)),
                  pl.BlockSpec((bs, GDN_H, GDN_DK, GDN_DV), lambda i: (i, 0, 0, 0)),
                  const((CONV_W, GDN_CONV)), const((1, 128)), const((1, 128)), const((1, GDN_DV))],
        out_specs=[pl.BlockSpec((bs, GDN_V), lambda i: (i, 0)),
                   pl.BlockSpec((bs, 3 * GDN_CONV), lambda i: (i, 0)),
                   pl.BlockSpec((bs, GDN_H, GDN_DK, GDN_DV), lambda i: (i, 0, 0, 0))],
        out_shape=[jax.ShapeDtypeStruct((n, GDN_V), bf16),
                   jax.ShapeDtypeStruct((n, 3 * GDN_CONV), f32),
                   jax.ShapeDtypeStruct((n, GDN_H, GDN_DK, GDN_DV), f32)],
        scratch_shapes=[pltpu.VMEM((bs, GDN_V), f32)],
        compiler_params=_cp(("parallel",)),
        name="gdn_step",
    )(proj, proj, proj, buf, state, w["gdn_cw"], w["gdn_dtb"], w["gdn_alog"], w["gdn_nw"])


def _outproj_kernel(ys_ref, og_ref, x_ref, g1_ref, sh_ref, sc_ref, nw_ref, w_ref, x1_ref, hf_ref):
    m = _dot(ys_ref[...], w_ref[0:SSD_INNER, :]) + _dot(og_ref[...], w_ref[SSD_INNER:, :])
    x1 = x_ref[...] + g1_ref[...] * m
    x1_ref[...] = x1
    y = x1 * lax.rsqrt(jnp.mean(x1 * x1, axis=-1, keepdims=True) + EPS) * nw_ref[...]
    hf_ref[...] = (y * (1.0 + sc_ref[...]) + sh_ref[...]).astype(hf_ref.dtype)


def _outproj(ys, og, x, mod, norm_w, w_out_b, per_token, rows_per_seq):
    t = x.shape[0]
    tm = min(512, t)
    return pl.pallas_call(
        _outproj_kernel,
        grid=(t // tm,),
        in_specs=[pl.BlockSpec((tm, SSD_INNER), lambda i: (i, 0)),
                  pl.BlockSpec((tm, GDN_V), lambda i: (i, 0)),
                  pl.BlockSpec((tm, D), lambda i: (i, 0)),
                  _mod_specs(per_token, tm, rows_per_seq, 2, 1),
                  _mod_specs(per_token, tm, rows_per_seq, 3, 1),
                  _mod_specs(per_token, tm, rows_per_seq, 4, 1),
                  pl.BlockSpec((1, D), lambda i: (0, 0)),
                  pl.BlockSpec((D, D), lambda i: (0, 0))],
        out_specs=[pl.BlockSpec((tm, D), lambda i: (i, 0)), pl.BlockSpec((tm, D), lambda i: (i, 0))],
        out_shape=[jax.ShapeDtypeStruct((t, D), f32), jax.ShapeDtypeStruct((t, D), bf16)],
        compiler_params=_cp(("parallel",)),
        name="outproj",
    )(ys, og, x, mod, mod, mod, norm_w.reshape(1, D), w_out_b)


def _peer_scores_kernel(hf_ref, wq_ref, sk_ref, o_ref):
    q_t = _dot_nt(wq_ref[...], hf_ref[...])
    half = PEER_DK // 2
    for hs in range(2 * PEER_H):
        o_ref[hs] = _dot(sk_ref[hs], q_t[hs * half:(hs + 1) * half].astype(bf16))


def _peer_scores(hf, wq_t, sk, tm):
    t = hf.shape[0]
    return pl.pallas_call(
        _peer_scores_kernel,
        grid=(t // tm,),
        in_specs=[pl.BlockSpec((tm, D), lambda i: (i, 0)),
                  pl.BlockSpec((PEER_H * PEER_DK, D), lambda i: (0, 0)),
                  pl.BlockSpec((2 * PEER_H, PEER_NK, PEER_DK // 2), lambda i: (0, 0, 0))],
        out_specs=pl.BlockSpec((2 * PEER_H, PEER_NK, tm), lambda i: (0, 0, i)),
        out_shape=jax.ShapeDtypeStruct((2 * PEER_H, PEER_NK, t), f32),
        compiler_params=_cp(("parallel",)),
        name="peer_scores",
    )(hf, wq_t, sk)


_STAIR = [PEER_K // (k1 + 1) for k1 in range(PEER_K)]
_STAIR_ROWS = -(-sum(_STAIR) // 8) * 8


def _peer_select_kernel(s_ref, thr_ref, f0_ref, e1_ref, v1_scr, cand):
    tm = s_ref.shape[-1]
    for h in range(PEER_H):
        s0 = s_ref[2 * h]
        s1 = s_ref[2 * h + 1]
        tops = []
        for cur in (s0, s1):
            vals = []
            for _ in range(PEER_K):
                m = jnp.max(cur, axis=0, keepdims=True)
                vals.append(m)
                cur = jnp.where(cur == m, -jnp.inf, cur)
            tops.append(vals)
        for k in range(PEER_K):
            v1_scr[k:k + 1, :] = tops[1][k]
        cand[_STAIR_ROWS - 8:_STAIR_ROWS, :] = jnp.full((8, tm), -jnp.inf, f32)
        off = 0
        for k1 in range(PEER_K):
            cand[off:off + _STAIR[k1], :] = tops[0][k1] + v1_scr[0:_STAIR[k1], :]
            off += _STAIR[k1]
        cv = cand[...]
        cur = cv
        for _ in range(PEER_K):
            tau = jnp.max(cur, axis=0, keepdims=True)
            cur = jnp.where(cur == tau, -jnp.inf, cur)
        m0, m1 = tops[0][0], tops[1][0]
        zsum = jnp.sum(jnp.where(cv >= tau, jnp.exp(cv - (m0 + m1)), 0.0), axis=0, keepdims=True)
        thr = jnp.full((PEER_NK, tm), jnp.inf, f32)
        for k1 in range(PEER_K):
            v1 = v1_scr[0:_STAIR[k1], :]
            thr_k = jnp.min(jnp.where(tops[0][k1] + v1 >= tau, v1, jnp.inf), axis=0, keepdims=True)
            thr = jnp.where(s0 == tops[0][k1], thr_k, thr)
        thr_ref[h] = thr
        f0_ref[h] = jnp.exp(s0 - m0) / zsum
        e1_ref[h] = jnp.exp(s1 - m1)


def _peer_select(scores_t, tm):
    t = scores_t.shape[-1]
    spec = pl.BlockSpec((PEER_H, PEER_NK, tm), lambda i: (0, 0, i))
    shape = jax.ShapeDtypeStruct((PEER_H, PEER_NK, t), f32)
    return pl.pallas_call(
        _peer_select_kernel,
        grid=(t // tm,),
        in_specs=[pl.BlockSpec((2 * PEER_H, PEER_NK, tm), lambda i: (0, 0, i))],
        out_specs=[spec, spec, spec],
        out_shape=[shape, shape, shape],
        scratch_shapes=[pltpu.VMEM((PEER_K, tm), f32), pltpu.VMEM((_STAIR_ROWS, tm), f32)],
        compiler_params=_cp(("parallel",)),
        name="peer_select",
    )(scores_t)


def _gelu(x):
    return 0.5 * x * (1.0 + lax.erf(x * (1.0 / math.sqrt(2.0))))


def _peer_dense_kernel(hf_ref, u_ref, v_ref, s1_ref, thr_ref, f0_ref, e1_ref, o_ref, cd):
    j = pl.program_id(1)

    @pl.when(j == 0)
    def _():
        o_ref[...] = jnp.zeros(o_ref.shape, f32)

    a_t = _dot_nt(u_ref[...], hf_ref[...])
    for i1 in range(u_ref.shape[0] // PEER_NK):
        wb = None
        for h in range(PEER_H):
            term = jnp.where(s1_ref[h] >= thr_ref[h, i1:i1 + 1, :], e1_ref[h], 0.0) * f0_ref[h, i1:i1 + 1, :]
            wb = term if wb is None else wb + term
        rows = slice(i1 * PEER_NK, (i1 + 1) * PEER_NK)
        cd[rows, :] = (wb * _gelu(a_t[rows, :])).astype(bf16)
    o_ref[...] += lax.dot_general(cd[...], v_ref[...], (((0,), (0,)), ((), ())), preferred_element_type=f32)


def _peer_dense(hf, u_b, v_b, scores_t, thr, f0, e1, tm, te):
    t = hf.shape[0]
    ni1 = te // PEER_NK
    assert ni1 == 8, "one 8-row block of side-0 rows per expert tile"
    s_view = scores_t.reshape(PEER_H, 2 * PEER_NK, t)
    rows8 = pl.BlockSpec((PEER_H, ni1, tm), lambda i, j: (0, j, i))
    return pl.pallas_call(
        _peer_dense_kernel,
        grid=(t // tm, PEER_E // te),
        in_specs=[pl.BlockSpec((tm, D), lambda i, j: (i, 0)),
                  pl.BlockSpec((te, D), lambda i, j: (j, 0)),
                  pl.BlockSpec((te, D), lambda i, j: (j, 0)),
                  pl.BlockSpec((PEER_H, PEER_NK, tm), lambda i, j: (0, 1, i)),
                  rows8, rows8,
                  pl.BlockSpec((PEER_H, PEER_NK, tm), lambda i, j: (0, 0, i))],
        out_specs=pl.BlockSpec((tm, D), lambda i, j: (i, 0)),
        out_shape=jax.ShapeDtypeStruct((t, D), f32),
        scratch_shapes=[pltpu.VMEM((te, tm), bf16)],
        compiler_params=_cp(("parallel", "arbitrary")),
        name="peer_dense",
    )(hf, u_b, v_b, s_view, thr, f0, e1)


def _final_kernel(x1_ref, p_ref, g2_ref, nw_ref, o_ref):
    x2 = x1_ref[...] + g2_ref[...] * p_ref[...]
    o_ref[...] = x2 * lax.rsqrt(jnp.mean(x2 * x2, axis=-1, keepdims=True) + EPS) * nw_ref[...]


def _final(x1, peer_out, mod, norm_w, per_token, rows_per_seq, row_off):
    t = x1.shape[0]
    tm = min(512, t)
    off = row_off // tm
    return pl.pallas_call(
        _final_kernel,
        grid=(t // tm,),
        in_specs=[pl.BlockSpec((tm, D), lambda i: (i, 0)),
                  pl.BlockSpec((tm, D), lambda i: (i + off, 0)),
                  _mod_specs(per_token, tm, rows_per_seq, 5, 1),
                  pl.BlockSpec((1, D), lambda i: (0, 0))],
        out_specs=pl.BlockSpec((tm, D), lambda i: (i, 0)),
        out_shape=jax.ShapeDtypeStruct((t, D), f32),
        compiler_params=_cp(("parallel",)),
        name="final_norm",
    )(x1, peer_out, mod, norm_w.reshape(1, D))


def _pad_row(v, lo):
    return jnp.zeros((1, 128), f32).at[0, lo:lo + v.shape[0]].set(v.astype(f32))


def _prep_weights(w_in, ssd_conv_w, ssd_conv_b, ssd_dt_bias, ssd_A_log, ssd_D, ssd_norm_w, gdn_conv_w,
                  gdn_dt_bias, gdn_A_log, gdn_norm_w):
    o_z, o_xbc = 0, SSD_INNER
    o_dt = o_xbc + SSD_CONV
    o_qkv = o_dt + SSD_H
    o_a = o_qkv + GDN_CONV
    o_b = o_a + GDN_H
    o_gate = o_b + GDN_H
    cols = [w_in[:, o_z:o_z + SSD_INNER], w_in[:, o_xbc:o_xbc + SSD_INNER], w_in[:, o_gate:o_gate + GDN_V],
            w_in[:, o_qkv:o_qkv + GDN_CONV], w_in[:, o_xbc + SSD_INNER:o_xbc + SSD_CONV],
            w_in[:, o_dt:o_dt + SSD_H], w_in[:, o_a:o_a + GDN_H], w_in[:, o_b:o_b + GDN_H]]
    w_re = jnp.concatenate(cols, axis=1)
    w_re = jnp.pad(w_re, ((0, 0), (0, P_COLS - w_re.shape[1]))).astype(bf16)
    expand = (jnp.arange(128)[:, None] == (jnp.arange(SSD_INNER)[None, :] // SSD_P)).astype(f32)
    return dict(
        w_in_re=w_re,
        ssd_wxs=ssd_conv_w[:, :SSD_INNER], ssd_wbc=ssd_conv_w[:, SSD_INNER:],
        ssd_bxs=ssd_conv_b[None, :SSD_INNER], ssd_bbc=ssd_conv_b[None, SSD_INNER:],
        ssd_cw=ssd_conv_w, ssd_cb=ssd_conv_b[None, :],
        ssd_dtb=_pad_row(ssd_dt_bias, SM_DT), ssd_alog=_pad_row(ssd_A_log, SM_DT),
        ssd_de=jnp.repeat(ssd_D.astype(f32), SSD_P)[None, :], ssd_nw=ssd_norm_w[None, :].astype(f32),
        ssd_expand=expand,
        gdn_cw=gdn_conv_w, gdn_dtb=_pad_row(gdn_dt_bias, SM_A), gdn_alog=_pad_row(gdn_A_log, SM_A),
        gdn_nw=gdn_norm_w[None, :].astype(f32),
    )


PEER_TM = 512
PEER_TE = 1024


def kernel(x_prompt, x_sample, c_prompt, c_sample, state_ssd, state_ssd_conv, state_gdn, state_gdn_conv, w_ada, b_ada, norm1_w, norm2_w, w_in, ssd_conv_w, ssd_conv_b, ssd_dt_bias, ssd_A_log, ssd_D, ssd_norm_w, gdn_conv_w, gdn_dt_bias, gdn_A_log, gdn_norm_w, w_out, peer_w_q, peer_sub_keys, peer_u, peer_v, final_norm_w):
    assert w_ada.shape[0] == 1, "single layer"
    nb, seq, _ = x_prompt.shape
    ns = x_sample.shape[0]
    assert x_sample.shape[1] == 1 and seq % SSD_CHUNK == 0 and seq % 512 == 0 and ns % 8 == 0
    w = _prep_weights(w_in[0], ssd_conv_w[0], ssd_conv_b[0], ssd_dt_bias[0], ssd_A_log[0], ssd_D[0], ssd_norm_w[0],
                      gdn_conv_w[0], gdn_dt_bias[0], gdn_A_log[0], gdn_norm_w[0])
    w_out_b = w_out[0].astype(bf16)
    wq_t = peer_w_q[0].T.astype(bf16)
    sk = peer_sub_keys[0].reshape(2 * PEER_H, PEER_NK, PEER_DK // 2).astype(bf16)
    u_b = peer_u[0].astype(bf16)
    v_b = peer_v[0].astype(bf16)

    nbp = -(-nb // 8) * 8
    c_all = jnp.concatenate([jnp.pad(c_prompt, ((0, nbp - nb), (0, 0))), c_sample], axis=0)
    mod = _modulation(c_all, w_ada[0], b_ada[0])
    mod_p = mod[:nb].reshape(nb, 1, 6 * D)
    mod_s = mod[nbp:]

    xp = x_prompt.reshape(nb * seq, D)
    xs = x_sample.reshape(ns, D)

    proj_p = _inproj(xp, mod_p, norm1_w[0], w["w_in_re"], False, seq)
    y_ssd_p, ssd_state_p = _ssd_prompt(proj_p, nb, seq, w)
    o_gdn_p, gdn_state_p = _gdn_prompt(proj_p, nb, seq, w)
    x1_p, hf_p = _outproj(y_ssd_p, o_gdn_p, xp, mod_p, norm2_w[0], w_out_b, False, seq)
    tail = proj_p.reshape(nb, seq, P_COLS)[:, seq - (CONV_W - 1):, :]
    ssd_conv_p = jnp.concatenate([tail[..., OFF_XS:OFF_XS + SSD_INNER], tail[..., OFF_BC:OFF_BC + SSD_BC]], axis=-1)
    gdn_conv_p = tail[..., OFF_QKV:OFF_QKV + GDN_CONV]

    proj_s = _inproj(xs, mod_s, norm1_w[0], w["w_in_re"], True, 1)
    y_ssd_s, ssd_conv_s, ssd_state_s = _ssd_step(proj_s, state_ssd_conv[0].reshape(ns, 3 * SSD_CONV), state_ssd[0], w)
    o_gdn_s, gdn_conv_s, gdn_state_s = _gdn_step(proj_s, state_gdn_conv[0].reshape(ns, 3 * GDN_CONV), state_gdn[0], w)
    x1_s, hf_s = _outproj(y_ssd_s, o_gdn_s, xs, mod_s, norm2_w[0], w_out_b, True, 1)

    tm = PEER_TM
    n_tok = nb * seq + ns
    pad = -n_tok % tm
    hf = jnp.concatenate([hf_p, hf_s, jnp.zeros((pad, D), bf16)], axis=0)
    scores_t = _peer_scores(hf, wq_t, sk, tm)
    thr, f0, e1 = _peer_select(scores_t, tm)
    peer_out = _peer_dense(hf, u_b, v_b, scores_t, thr, f0, e1, tm, PEER_TE)

    y_p = _final(x1_p, peer_out, mod_p, final_norm_w, False, seq, 0)
    y_s = _final(x1_s, peer_out, mod_s, final_norm_w, True, 1, nb * seq)

    return (y_p.reshape(nb, seq, D), y_s.reshape(ns, 1, D),
            ssd_state_p[None], ssd_conv_p[None], gdn_state_p[None], gdn_conv_p[None],
            ssd_state_s[None], ssd_conv_s.reshape(1, ns, CONV_W - 1, SSD_CONV),
            gdn_state_s[None], gdn_conv_s.reshape(1, ns, CONV_W - 1, GDN_CONV))
```

```python
import functools
import math

import jax
import jax.numpy as jnp
from jax import lax
from jax.experimental import pallas as pl
from jax.experimental.pallas import tpu as pltpu

f32 = jnp.float32
bf16 = jnp.bfloat16

D = 2048
CONV_W = 4
SSD_P = 64
SSD_INNER = D // 2
SSD_H = SSD_INNER // SSD_P
SSD_G = 2
SSD_N = 128
SSD_CHUNK = 128
SSD_BC = 2 * SSD_G * SSD_N
SSD_CONV = SSD_INNER + SSD_BC
GDN_DK = 128
GDN_DV = 128
GDN_V = D // 2
GDN_H = GDN_V // GDN_DV
GDN_CHUNK = 64
GDN_CONV = GDN_H * (2 * GDN_DK + GDN_DV)
PEER_H = 8
PEER_NK = 128
PEER_E = PEER_NK * PEER_NK
PEER_DK = 256
PEER_K = 16
EPS = 1e-6
NEG = -1e30

OFF_Z, OFF_XS, OFF_GATE, OFF_QKV, OFF_BC, OFF_SM = 0, 1024, 2048, 3072, 6144, 6656
P_COLS = 7168
SM_DT, SM_A, SM_B = 0, 16, 24

VMEM_LIMIT = 56 * 1024 * 1024


def _cp(sem):
    return pltpu.CompilerParams(dimension_semantics=sem, vmem_limit_bytes=VMEM_LIMIT)


def _silu(x):
    return x * jax.nn.sigmoid(x)


def _softplus(x):
    return jnp.maximum(x, 0.0) + jnp.log1p(jnp.exp(-jnp.abs(x)))


def _dot(a, b):
    return jnp.dot(a, b, preferred_element_type=f32)


def _dot_nt(a, b):
    return lax.dot_general(a, b, (((1,), (1,)), ((), ())), preferred_element_type=f32)


def _dot_hi(a, b):
    return jnp.dot(a, b, preferred_element_type=f32, precision=lax.Precision.HIGHEST)


def _iota(shape, dim):
    return lax.broadcasted_iota(jnp.int32, shape, dim)


def _mod_kernel(c_ref, w_ref, b_ref, o_ref):
    a = _silu(c_ref[...]).astype(bf16)
    o_ref[...] = _dot(a, w_ref[...].astype(bf16)) + b_ref[...]


def _modulation(c_all, w_ada, b_ada):
    m, tn = c_all.shape[0], 1024
    return pl.pallas_call(
        _mod_kernel,
        grid=(6 * D // tn,),
        in_specs=[pl.BlockSpec((m, D), lambda j: (0, 0)),
                  pl.BlockSpec((D, tn), lambda j: (0, j)),
                  pl.BlockSpec((1, tn), lambda j: (0, j))],
        out_specs=pl.BlockSpec((m, tn), lambda j: (0, j)),
        out_shape=jax.ShapeDtypeStruct((m, 6 * D), f32),
        compiler_params=_cp(("arbitrary",)),
        name="adaln_mod",
    )(c_all, w_ada, b_ada.reshape(1, 6 * D))


def _mod_specs(per_token, tm, rows_per_seq, which, grid_rank):
    if per_token:
        if grid_rank == 2:
            return pl.BlockSpec((tm, D), lambda i, j: (i, which))
        return pl.BlockSpec((tm, D), lambda i: (i, which))
    tps = rows_per_seq // tm
    if grid_rank == 2:
        return pl.BlockSpec((None, 1, D), lambda i, j: (i // tps, 0, which))
    return pl.BlockSpec((None, 1, D), lambda i: (i // tps, 0, which))


def _inproj_kernel(x_ref, sh_ref, sc_ref, nw_ref, w_ref, o_ref, hm_ref):
    @pl.when(pl.program_id(1) == 0)
    def _():
        x = x_ref[...]
        y = x * lax.rsqrt(jnp.mean(x * x, axis=-1, keepdims=True) + EPS) * nw_ref[...]
        hm_ref[...] = (y * (1.0 + sc_ref[...]) + sh_ref[...]).astype(bf16)

    o_ref[...] = _dot(hm_ref[...], w_ref[...])


def _inproj(x, mod, norm_w, w_re, per_token, rows_per_seq):
    t = x.shape[0]
    tm = min(1024, t)
    assert per_token or rows_per_seq % tm == 0
    tn = 1024
    return pl.pallas_call(
        _inproj_kernel,
        grid=(t // tm, P_COLS // tn),
        in_specs=[pl.BlockSpec((tm, D), lambda i, j: (i, 0)),
                  _mod_specs(per_token, tm, rows_per_seq, 0, 2),
                  _mod_specs(per_token, tm, rows_per_seq, 1, 2),
                  pl.BlockSpec((1, D), lambda i, j: (0, 0)),
                  pl.BlockSpec((D, tn), lambda i, j: (0, j))],
        out_specs=pl.BlockSpec((tm, tn), lambda i, j: (i, j)),
        out_shape=jax.ShapeDtypeStruct((t, P_COLS), f32),
        scratch_shapes=[pltpu.VMEM((tm, D), bf16)],
        compiler_params=_cp(("parallel", "arbitrary")),
        name="inproj",
    )(x, mod, mod, norm_w.reshape(1, D), w_re)


def _conv_chunk(ext_ref, cur, w_ref, b_row, q):
    ext_ref[8:8 + q, :] = cur
    acc = w_ref[CONV_W - 1:CONV_W, :] * cur
    for k in range(CONV_W - 1):
        acc = acc + w_ref[k:k + 1, :] * ext_ref[5 + k:5 + k + q, :]
    ext_ref[0:8, :] = cur[q - 8:q, :]
    if b_row is not None:
        acc = acc + b_row
    return acc


def _group_rmsnorm(y, nw, width):
    outs = []
    for g in range(y.shape[-1] // width):
        seg = y[:, g * width:(g + 1) * width]
        ms = jnp.mean(seg * seg, axis=-1, keepdims=True)
        outs.append(seg * lax.rsqrt(ms + EPS) * nw[:, g * width:(g + 1) * width])
    return jnp.concatenate(outs, axis=-1)


def _lane_mask(shape, lo, hi):
    lane = _iota(shape, len(shape) - 1)
    return (lane >= lo) & (lane < hi)


def _ssd_kernel(z_ref, xs_ref, bc_ref, sm_ref, wxs_ref, wbc_ref, bxs_ref, bbc_ref, dtb_ref, alog_ref, de_ref,
                nw_ref, y_ref, st_ref, extx, extb, ht, yscr):
    q = SSD_CHUNK
    c = pl.program_id(1)

    @pl.when(c == 0)
    def _():
        extx[0:8, :] = jnp.zeros((8, SSD_INNER), f32)
        extb[0:8, :] = jnp.zeros((8, SSD_BC), f32)
        ht[...] = jnp.zeros(ht.shape, f32)

    xs = _silu(_conv_chunk(extx, xs_ref[...], wxs_ref, bxs_ref[...], q))
    bcv = _silu(_conv_chunk(extb, bc_ref[...], wbc_ref, bbc_ref[...], q))

    head_lanes = _lane_mask((q, 128), SM_DT, SM_DT + SSD_H)
    dt = jnp.where(head_lanes, _softplus(sm_ref[...] + dtb_ref[...]), 0.0)
    a_row = -jnp.exp(alog_ref[...])
    tri = (_iota((q, q), 0) >= _iota((q, q), 1))
    cs = _dot_hi(tri.astype(f32), dt * a_row)
    cs_t = cs.T
    dt_t = dt.T
    last_col = cs_t[:, q - 1:q]
    w_t = dt_t * jnp.exp(last_col - cs_t)
    ecs = jnp.exp(cs)
    lane_lo = _iota((q, 128), 1) < SSD_P

    for g in range(SSD_G):
        b_g = bcv[:, g * SSD_N:(g + 1) * SSD_N]
        c_g = bcv[:, SSD_G * SSD_N + g * SSD_N:SSD_G * SSD_N + (g + 1) * SSD_N]
        b_gt = b_g.T
        cb = _dot(c_g.astype(bf16), b_gt.astype(bf16))
        for r2 in range(SSD_H // SSD_G // 2):
            pair = g * (SSD_H // SSD_G // 2) + r2
            xs_pair = xs[:, pair * 128:(pair + 1) * 128]
            xs_pair_b = xs_pair.astype(bf16)
            h_prev = ht[pair]
            rhs = jnp.concatenate([xs_pair_b, h_prev.astype(bf16)], axis=0)
            ys, hs, els = [], [], []
            for e in range(2):
                h = 2 * pair + e
                diff = cs[:, h:h + 1] - cs_t[h:h + 1, :]
                dec = jnp.exp(jnp.where(tri, diff, NEG))
                m = cb * dec * dt_t[h:h + 1, :]
                lhs = jnp.concatenate([m, c_g * ecs[:, h:h + 1]], axis=1).astype(bf16)
                ys.append(_dot(lhs, rhs))
                bw = (b_gt * w_t[h:h + 1, :]).astype(bf16)
                hs.append(_dot(bw, xs_pair_b))
                els.append(jnp.exp(last_col[h:h + 1, :]))
            yscr[:, pair * 128:(pair + 1) * 128] = jnp.where(lane_lo, ys[0], ys[1])
            ht[pair] = h_prev * jnp.where(lane_lo, els[0], els[1]) + jnp.where(lane_lo, hs[0], hs[1])

    z = z_ref[...]
    y = (yscr[...] + de_ref[...] * xs) * _silu(z)
    y_ref[...] = _group_rmsnorm(y, nw_ref[...], SSD_INNER // SSD_G).astype(y_ref.dtype)

    @pl.when(c == pl.num_programs(1) - 1)
    def _():
        for pair in range(SSD_H // 2):
            t = ht[pair].T
            st_ref[2 * pair] = t[0:SSD_P]
            st_ref[2 * pair + 1] = t[SSD_P:2 * SSD_P]


def _ssd_prompt(proj, nb, seq, w):
    nc = seq // SSD_CHUNK
    q = SSD_CHUNK
    row = lambda n, c: n * nc + c
    const = lambda shape: pl.BlockSpec(shape, lambda n, c: (0, 0))
    return pl.pallas_call(
        _ssd_kernel,
        grid=(nb, nc),
        in_specs=[pl.BlockSpec((q, 1024), lambda n, c: (row(n, c), OFF_Z // 1024)),
                  pl.BlockSpec((q, 1024), lambda n, c: (row(n, c), OFF_XS // 1024)),
                  pl.BlockSpec((q, SSD_BC), lambda n, c: (row(n, c), OFF_BC // SSD_BC)),
                  pl.BlockSpec((q, 128), lambda n, c: (row(n, c), OFF_SM // 128)),
                  const((CONV_W, SSD_INNER)), const((CONV_W, SSD_BC)), const((1, SSD_INNER)), const((1, SSD_BC)),
                  const((1, 128)), const((1, 128)), const((1, SSD_INNER)), const((1, SSD_INNER))],
        out_specs=[pl.BlockSpec((q, SSD_INNER), lambda n, c: (row(n, c), 0)),
                   pl.BlockSpec((None, SSD_H, SSD_P, SSD_N), lambda n, c: (n, 0, 0, 0))],
        out_shape=[jax.ShapeDtypeStruct((nb * seq, SSD_INNER), bf16),
                   jax.ShapeDtypeStruct((nb, SSD_H, SSD_P, SSD_N), f32)],
        scratch_shapes=[pltpu.VMEM((q + 8, SSD_INNER), f32), pltpu.VMEM((q + 8, SSD_BC), f32),
                        pltpu.VMEM((SSD_H // 2, SSD_N, 2 * SSD_P), f32), pltpu.VMEM((q, SSD_INNER), f32)],
        compiler_params=_cp(("parallel", "arbitrary")),
        name="ssd_scan",
    )(proj, proj, proj, proj, w["ssd_wxs"], w["ssd_wbc"], w["ssd_bxs"], w["ssd_bbc"], w["ssd_dtb"],
      w["ssd_alog"], w["ssd_de"], w["ssd_nw"])


def _gdn_kernel(qkv_ref, gate_ref, sm_ref, cw_ref, dtb_ref, alog_ref, nw_ref, o_ref, st_ref, ext, s_scr):
    q = GDN_CHUNK
    c = pl.program_id(1)

    @pl.when(c == 0)
    def _():
        ext[0:8, :] = jnp.zeros((8, GDN_CONV), f32)
        s_scr[...] = jnp.zeros(s_scr.shape, f32)

    act = _silu(_conv_chunk(ext, qkv_ref[...], cw_ref, None, q))
    sm = sm_ref[...]
    a_lanes = _lane_mask((q, 128), SM_A, SM_A + GDN_H)
    g = jnp.where(a_lanes, -jnp.exp(alog_ref[...]) * _softplus(sm + dtb_ref[...]), 0.0)
    beta = jax.nn.sigmoid(sm)
    incl = _iota((q, q), 0) >= _iota((q, q), 1)
    strict = _iota((q, q), 0) > _iota((q, q), 1)
    gc = _dot_hi(incl.astype(f32), g)
    gc_t = gc.T
    egc = jnp.exp(gc)
    gate = gate_ref[...]
    nw = nw_ref[...]

    heads = range(GDN_H)
    rb, cb_ = _iota((q, q), 0), _iota((q, q), 1)
    same16 = (rb // 16) == (cb_ // 16)
    same32 = (rb // 32) == (cb_ // 32)
    diag_mask = strict & same16
    off32_mask = strict & same32 & jnp.logical_not(same16)
    off64_mask = strict & jnp.logical_not(same32)

    qn, kn, kb, dec, ecol, bcol, gcol, vh = [], [], [], [], [], [], [], []
    for h in heads:
        qh = act[:, h * GDN_DK:(h + 1) * GDN_DK]
        kh = act[:, GDN_H * GDN_DK + h * GDN_DK:GDN_H * GDN_DK + (h + 1) * GDN_DK]
        vh.append(act[:, 2 * GDN_H * GDN_DK + h * GDN_DV:2 * GDN_H * GDN_DK + (h + 1) * GDN_DV])
        qn.append(qh * lax.rsqrt(jnp.sum(qh * qh, axis=-1, keepdims=True) + EPS) * (GDN_DK ** -0.5))
        kn.append(kh * lax.rsqrt(jnp.sum(kh * kh, axis=-1, keepdims=True) + EPS))
        gcol.append(gc[:, SM_A + h:SM_A + h + 1])
        grow = gc_t[SM_A + h:SM_A + h + 1, :]
        ecol.append(egc[:, SM_A + h:SM_A + h + 1])
        bcol.append(beta[:, SM_B + h:SM_B + h + 1])
        dec.append(jnp.exp(jnp.where(incl, gcol[h] - grow, NEG)))
        kb.append(kn[h] * bcol[h])
    kn_b = [kn[h].astype(bf16) for h in heads]
    lmat = [_dot_nt(kb[h].astype(bf16), kn_b[h]) * dec[h] for h in heads]
    qk = [_dot_nt(qn[h].astype(bf16), kn_b[h]) * dec[h] for h in heads]

    dotb = lambda a, b: _dot(a.astype(bf16), b.astype(bf16))
    n_pow = [jnp.where(diag_mask, -lmat[h], 0.0) for h in heads]
    t_m = list(n_pow)
    for _ in range(3):
        n_pow = [dotb(n_pow[h], n_pow[h]) for h in heads]
        t_m = [t_m[h] + n_pow[h] + dotb(t_m[h], n_pow[h]) for h in heads]
    for mask in (off32_mask, off64_mask):
        off = [jnp.where(mask, lmat[h], 0.0) for h in heads]
        y = [off[h] + dotb(off[h], t_m[h]) for h in heads]
        t_m = [t_m[h] - (y[h] + dotb(t_m[h], y[h])) for h in heads]

    rhs = [jnp.concatenate([vh[h] * bcol[h], kb[h] * ecol[h]], axis=1) for h in heads]
    sol = [rhs[h] + dotb(t_m[h], rhs[h]) for h in heads]
    s_prev = [s_scr[h] for h in heads]
    s_b = [s_prev[h].astype(bf16) for h in heads]
    v_new = [sol[h][:, :GDN_DV] - _dot(sol[h][:, GDN_DV:].astype(bf16), s_b[h]) for h in heads]
    v_new_b = [v_new[h].astype(bf16) for h in heads]
    o = [_dot((qn[h] * ecol[h]).astype(bf16), s_b[h]) + _dot(qk[h].astype(bf16), v_new_b[h]) for h in heads]
    for h in heads:
        glast = gcol[h][q - 1:q, :]
        kdec = kn[h] * jnp.exp(glast - gcol[h])
        s_scr[h] = s_prev[h] * jnp.exp(glast) + _dot(kdec.T.astype(bf16), v_new_b[h])
    for h in heads:
        on = o[h] * lax.rsqrt(jnp.mean(o[h] * o[h], axis=-1, keepdims=True) + EPS) * nw
        o_ref[:, h * GDN_DV:(h + 1) * GDN_DV] = (on * _silu(gate[:, h * GDN_DV:(h + 1) * GDN_DV])).astype(o_ref.dtype)

    @pl.when(c == pl.num_programs(1) - 1)
    def _():
        st_ref[...] = s_scr[...]


def _gdn_prompt(proj, nb, seq, w):
    nc = seq // GDN_CHUNK
    q = GDN_CHUNK
    row = lambda n, c: n * nc + c
    const = lambda shape: pl.BlockSpec(shape, lambda n, c: (0, 0))
    return pl.pallas_call(
        _gdn_kernel,
        grid=(nb, nc),
        in_specs=[pl.BlockSpec((q, GDN_CONV), lambda n, c: (row(n, c), OFF_QKV // GDN_CONV)),
                  pl.BlockSpec((q, GDN_V), lambda n, c: (row(n, c), OFF_GATE // GDN_V)),
                  pl.BlockSpec((q, 128), lambda n, c: (row(n, c), OFF_SM // 128)),
                  const((CONV_W, GDN_CONV)), const((1, 128)), const((1, 128)), const((1, GDN_DV))],
        out_specs=[pl.BlockSpec((q, GDN_V), lambda n, c: (row(n, c), 0)),
                   pl.BlockSpec((None, GDN_H, GDN_DK, GDN_DV), lambda n, c: (n, 0, 0, 0))],
        out_shape=[jax.ShapeDtypeStruct((nb * seq, GDN_V), bf16),
                   jax.ShapeDtypeStruct((nb, GDN_H, GDN_DK, GDN_DV), f32)],
        scratch_shapes=[pltpu.VMEM((q + 8, GDN_CONV), f32), pltpu.VMEM((GDN_H, GDN_DK, GDN_DV), f32)],
        compiler_params=_cp(("parallel", "arbitrary")),
        name="gdn_scan",
    )(proj, proj, proj, w["gdn_cw"], w["gdn_dtb"], w["gdn_alog"], w["gdn_nw"])


def _conv_step(buf, xraw, w_ref, width):
    acc = w_ref[CONV_W - 1:CONV_W, :] * xraw
    for k in range(CONV_W - 1):
        acc = acc + w_ref[k:k + 1, :] * buf[:, k * width:(k + 1) * width]
    return acc, jnp.concatenate([buf[:, width:], xraw], axis=1)


def _ssd_step_kernel(z_ref, xs_ref, bc_ref, sm_ref, buf_ref, st_ref, cw_ref, cb_ref, dtb_ref, alog_ref, de_ref,
                     nw_ref, ex_ref, y_ref, nbuf_ref, nst_ref):
    bs = z_ref.shape[0]
    xraw = jnp.concatenate([xs_ref[...], bc_ref[...]], axis=1)
    conv, nbuf = _conv_step(buf_ref[...], xraw, cw_ref, SSD_CONV)
    nbuf_ref[...] = nbuf
    act = _silu(conv + cb_ref[...])
    xs = act[:, :SSD_INNER]
    head_lanes = _lane_mask((bs, 128), SM_DT, SM_DT + SSD_H)
    dt = jnp.where(head_lanes, _softplus(sm_ref[...] + dtb_ref[...]), 0.0)
    da = jnp.exp(dt * (-jnp.exp(alog_ref[...])))
    dt_e = _dot_hi(dt, ex_ref[...])
    da_e = _dot_hi(da, ex_ref[...])
    cols = jnp.concatenate([xs * dt_e, da_e], axis=0).T
    lane = _iota((SSD_INNER, 128), 1)
    ycols = jnp.zeros((SSD_INNER, 128), f32)
    gw = SSD_INNER // SSD_G
    hg = SSD_H // SSD_G
    for b in range(bs):
        xcol = cols[:, b:b + 1]
        dcol = cols[:, bs + b:bs + b + 1]
        ys = []
        for g in range(SSD_G):
            b_row = act[b:b + 1, SSD_INNER + g * SSD_N:SSD_INNER + (g + 1) * SSD_N]
            c_row = act[b:b + 1, SSD_INNER + (SSD_G + g) * SSD_N:SSD_INNER + (SSD_G + g + 1) * SSD_N]
            st = st_ref[b, g * hg:(g + 1) * hg].reshape(gw, SSD_N)
            new = st * dcol[g * gw:(g + 1) * gw] + xcol[g * gw:(g + 1) * gw] * b_row
            nst_ref[b, g * hg:(g + 1) * hg] = new.reshape(hg, SSD_P, SSD_N)
            ys.append(jnp.sum(new * c_row, axis=-1, keepdims=True))
        ycols = jnp.where(lane == b, jnp.concatenate(ys, axis=0), ycols)
    y = ycols.T[0:bs]
    y = (y + de_ref[...] * xs) * _silu(z_ref[...])
    y_ref[...] = _group_rmsnorm(y, nw_ref[...], gw).astype(y_ref.dtype)


def _ssd_step(proj, buf, state, w, bs=8):
    n = proj.shape[0]
    const = lambda shape: pl.BlockSpec(shape, lambda i: (0,) * len(shape))
    return pl.pallas_call(
        _ssd_step_kernel,
        grid=(n // bs,),
        in_specs=[pl.BlockSpec((bs, 1024), lambda i: (i, OFF_Z // 1024)),
                  pl.BlockSpec((bs, 1024), lambda i: (i, OFF_XS // 1024)),
                  pl.BlockSpec((bs, SSD_BC), lambda i: (i, OFF_BC // SSD_BC)),
                  pl.BlockSpec((bs, 128), lambda i: (i, OFF_SM // 128)),
                  pl.BlockSpec((bs, 3 * SSD_CONV), lambda i: (i, 0)),
                  pl.BlockSpec((bs, SSD_H, SSD_P, SSD_N), lambda i: (i, 0, 0, 0)),
                  const((CONV_W, SSD_CONV)), const((1, SSD_CONV)), const((1, 128)), const((1, 128)),
                  const((1, SSD_INNER)), const((1, SSD_INNER)), const((128, SSD_INNER))],
        out_specs=[pl.BlockSpec((bs, SSD_INNER), lambda i: (i, 0)),
                   pl.BlockSpec((bs, 3 * SSD_CONV), lambda i: (i, 0)),
                   pl.BlockSpec((bs, SSD_H, SSD_P, SSD_N), lambda i: (i, 0, 0, 0))],
        out_shape=[jax.ShapeDtypeStruct((n, SSD_INNER), bf16),
                   jax.ShapeDtypeStruct((n, 3 * SSD_CONV), f32),
                   jax.ShapeDtypeStruct((n, SSD_H, SSD_P, SSD_N), f32)],
        compiler_params=_cp(("parallel",)),
        name="ssd_step",
    )(proj, proj, proj, proj, buf, state, w["ssd_cw"], w["ssd_cb"], w["ssd_dtb"], w["ssd_alog"], w["ssd_de"],
      w["ssd_nw"], w["ssd_expand"])


def _gdn_step_kernel(qkv_ref, gate_ref, sm_ref, buf_ref, st_ref, cw_ref, dtb_ref, alog_ref, nw_ref,
                     o_ref, nbuf_ref, nst_ref, oscr):
    bs = qkv_ref.shape[0]
    conv, nbuf = _conv_step(buf_ref[...], qkv_ref[...], cw_ref, GDN_CONV)
    nbuf_ref[...] = nbuf
    act = _silu(conv)
    sm = sm_ref[...]
    eg = jnp.exp(-jnp.exp(alog_ref[...]) * _softplus(sm + dtb_ref[...]))
    beta = jax.nn.sigmoid(sm)
    qs, ks = [], []
    for h in range(GDN_H):
        qh = act[:, h * GDN_DK:(h + 1) * GDN_DK]
        kh = act[:, GDN_H * GDN_DK + h * GDN_DK:GDN_H * GDN_DK + (h + 1) * GDN_DK]
        qs.append(qh * lax.rsqrt(jnp.sum(qh * qh, axis=-1, keepdims=True) + EPS) * (GDN_DK ** -0.5))
        ks.append(kh * lax.rsqrt(jnp.sum(kh * kh, axis=-1, keepdims=True) + EPS))
    qk_t = jnp.concatenate(qs + ks, axis=1).T
    for b in range(bs):
        for h in range(GDN_H):
            qcol = qk_t[h * GDN_DK:(h + 1) * GDN_DK, b:b + 1]
            kcol = qk_t[(GDN_H + h) * GDN_DK:(GDN_H + h + 1) * GDN_DK, b:b + 1]
            vrow = act[b:b + 1, 2 * GDN_H * GDN_DK + h * GDN_DV:2 * GDN_H * GDN_DK + (h + 1) * GDN_DV]
            egs = eg[b:b + 1, SM_A + h:SM_A + h + 1]
            bet = beta[b:b + 1, SM_B + h:SM_B + h + 1]
            s = st_ref[b, h]
            v_new = bet * (vrow - egs * jnp.sum(kcol * s, axis=0, keepdims=True))
            s_new = s * egs + kcol * v_new
            nst_ref[b, h] = s_new
            oscr[b:b + 1, h * GDN_DV:(h + 1) * GDN_DV] = jnp.sum(qcol * s_new, axis=0, keepdims=True)
    o = oscr[...]
    nw = nw_ref[...]
    gate = gate_ref[...]
    for h in range(GDN_H):
        oh = o[:, h * GDN_DV:(h + 1) * GDN_DV]
        on = oh * lax.rsqrt(jnp.mean(oh * oh, axis=-1, keepdims=True) + EPS) * nw
        o_ref[:, h * GDN_DV:(h + 1) * GDN_DV] = (on * _silu(gate[:, h * GDN_DV:(h + 1) * GDN_DV])).astype(o_ref.dtype)


def _gdn_step(proj, buf, state, w, bs=8):
    n = proj.shape[0]
    const = lambda shape: pl.BlockSpec(shape, lambda i: (0,) * len(shape))
    return pl.pallas_call(
        _gdn_step_kernel,
        grid=(n // bs,),
        in_specs=[pl.BlockSpec((bs, GDN_CONV), lambda i: (i, OFF_QKV // GDN_CONV)),
                  pl.BlockSpec((bs, GDN_V), lambda i: (i, OFF_GATE // GDN_V)),
                  pl.BlockSpec((bs, 128), lambda i: (i, OFF_SM // 128)),
                  pl.BlockSpec((bs, 3 * GDN_CONV), lambda i: (i, 0)),
                  pl.BlockSpec((bs, GDN_H, GDN_DK, GDN_DV), lambda i: (i, 0, 0, 0)),
                  const((CONV_W, GDN_CONV)), const((1, 128)), const((1, 128)), const((1, GDN_DV))],
        out_specs=[pl.BlockSpec((bs, GDN_V), lambda i: (i, 0)),
                   pl.BlockSpec((bs, 3 * GDN_CONV), lambda i: (i, 0)),
                   pl.BlockSpec((bs, GDN_H, GDN_DK, GDN_DV), lambda i: (i, 0, 0, 0))],
        out_shape=[jax.ShapeDtypeStruct((n, GDN_V), bf16),
                   jax.ShapeDtypeStruct((n, 3 * GDN_CONV), f32),
                   jax.ShapeDtypeStruct((n, GDN_H, GDN_DK, GDN_DV), f32)],
        scratch_shapes=[pltpu.VMEM((bs, GDN_V), f32)],
        compiler_params=_cp(("parallel",)),
        name="gdn_step",
    )(proj, proj, proj, buf, state, w["gdn_cw"], w["gdn_dtb"], w["gdn_alog"], w["gdn_nw"])


def _outproj_kernel(ys_ref, og_ref, x_ref, g1_ref, sh_ref, sc_ref, nw_ref, w_ref, x1_ref, hf_ref):
    m = _dot(ys_ref[...], w_ref[0:SSD_INNER, :]) + _dot(og_ref[...], w_ref[SSD_INNER:, :])
    x1 = x_ref[...] + g1_ref[...] * m
    x1_ref[...] = x1
    y = x1 * lax.rsqrt(jnp.mean(x1 * x1, axis=-1, keepdims=True) + EPS) * nw_ref[...]
    hf_ref[...] = (y * (1.0 + sc_ref[...]) + sh_ref[...]).astype(hf_ref.dtype)


def _outproj(ys, og, x, mod, norm_w, w_out_b, per_token, rows_per_seq):
    t = x.shape[0]
    tm = min(512, t)
    return pl.pallas_call(
        _outproj_kernel,
        grid=(t // tm,),
        in_specs=[pl.BlockSpec((tm, SSD_INNER), lambda i: (i, 0)),
                  pl.BlockSpec((tm, GDN_V), lambda i: (i, 0)),
                  pl.BlockSpec((tm, D), lambda i: (i, 0)),
                  _mod_specs(per_token, tm, rows_per_seq, 2, 1),
                  _mod_specs(per_token, tm, rows_per_seq, 3, 1),
                  _mod_specs(per_token, tm, rows_per_seq, 4, 1),
                  pl.BlockSpec((1, D), lambda i: (0, 0)),
                  pl.BlockSpec((D, D), lambda i: (0, 0))],
        out_specs=[pl.BlockSpec((tm, D), lambda i: (i, 0)), pl.BlockSpec((tm, D), lambda i: (i, 0))],
        out_shape=[jax.ShapeDtypeStruct((t, D), f32), jax.ShapeDtypeStruct((t, D), bf16)],
        compiler_params=_cp(("parallel",)),
        name="outproj",
    )(ys, og, x, mod, mod, mod, norm_w.reshape(1, D), w_out_b)


def _peer_scores_kernel(hf_ref, wq_ref, sk_ref, o_ref):
    q_t = _dot_nt(wq_ref[...], hf_ref[...])
    half = PEER_DK // 2
    for hs in range(2 * PEER_H):
        o_ref[hs] = _dot(sk_ref[hs], q_t[hs * half:(hs + 1) * half].astype(bf16))


def _peer_scores(hf, wq_t, sk, tm):
    t = hf.shape[0]
    return pl.pallas_call(
        _peer_scores_kernel,
        grid=(t // tm,),
        in_specs=[pl.BlockSpec((tm, D), lambda i: (i, 0)),
                  pl.BlockSpec((PEER_H * PEER_DK, D), lambda i: (0, 0)),
                  pl.BlockSpec((2 * PEER_H, PEER_NK, PEER_DK // 2), lambda i: (0, 0, 0))],
        out_specs=pl.BlockSpec((2 * PEER_H, PEER_NK, tm), lambda i: (0, 0, i)),
        out_shape=jax.ShapeDtypeStruct((2 * PEER_H, PEER_NK, t), f32),
        compiler_params=_cp(("parallel",)),
        name="peer_scores",
    )(hf, wq_t, sk)


_STAIR = [PEER_K // (k1 + 1) for k1 in range(PEER_K)]
_STAIR_ROWS = -(-sum(_STAIR) // 8) * 8


def _peer_select_kernel(s_ref, rank_ref, cnt_ref, f0_ref, e1_ref, v1_scr, cand):
    tm = s_ref.shape[-1]
    for h in range(PEER_H):
        s0 = s_ref[2 * h]
        s1 = s_ref[2 * h + 1]
        tops = []
        rank1 = jnp.full((PEER_NK, tm), float(PEER_K), f32)
        for side, cur in enumerate((s0, s1)):
            vals = []
            for k in range(PEER_K):
                m = jnp.max(cur, axis=0, keepdims=True)
                vals.append(m)
                hit = cur == m
                if side == 1:
                    rank1 = jnp.where(hit, float(k), rank1)
                cur = jnp.where(hit, -jnp.inf, cur)
            tops.append(vals)
        for k in range(PEER_K):
            v1_scr[k:k + 1, :] = tops[1][k]
        cand[_STAIR_ROWS - 8:_STAIR_ROWS, :] = jnp.full((8, tm), -jnp.inf, f32)
        off = 0
        for k1 in range(PEER_K):
            cand[off:off + _STAIR[k1], :] = tops[0][k1] + v1_scr[0:_STAIR[k1], :]
            off += _STAIR[k1]
        cv = cand[...]
        cur = cv
        for _ in range(PEER_K):
            tau = jnp.max(cur, axis=0, keepdims=True)
            cur = jnp.where(cur == tau, -jnp.inf, cur)
        m0, m1 = tops[0][0], tops[1][0]
        zsum = jnp.sum(jnp.where(cv >= tau, jnp.exp(cv - (m0 + m1)), 0.0), axis=0, keepdims=True)
        cnt = jnp.zeros((PEER_NK, tm), f32)
        for k1 in range(PEER_K):
            v1 = v1_scr[0:_STAIR[k1], :]
            n_k = jnp.sum(jnp.where(tops[0][k1] + v1 >= tau, 1.0, 0.0), axis=0, keepdims=True)
            cnt = jnp.where(s0 == tops[0][k1], n_k, cnt)
        rank_ref[h] = rank1.astype(bf16)
        cnt_ref[h] = cnt
        f0_ref[h] = jnp.exp(s0 - m0) / zsum
        e1_ref[h] = jnp.exp(s1 - m1).astype(bf16)


def _peer_select(scores_t, tm):
    t = scores_t.shape[-1]
    spec = pl.BlockSpec((PEER_H, PEER_NK, tm), lambda i: (0, 0, i))
    shape = lambda dt: jax.ShapeDtypeStruct((PEER_H, PEER_NK, t), dt)
    return pl.pallas_call(
        _peer_select_kernel,
        grid=(t // tm,),
        in_specs=[pl.BlockSpec((2 * PEER_H, PEER_NK, tm), lambda i: (0, 0, i))],
        out_specs=[spec, spec, spec, spec],
        out_shape=[shape(bf16), shape(f32), shape(f32), shape(bf16)],
        scratch_shapes=[pltpu.VMEM((PEER_K, tm), f32), pltpu.VMEM((_STAIR_ROWS, tm), f32)],
        compiler_params=_cp(("parallel",)),
        name="peer_select",
    )(scores_t)


def _gelu(x):
    return 0.5 * x * (1.0 + lax.erf(x * (1.0 / math.sqrt(2.0))))


def _peer_dense_kernel(hf_ref, u_ref, v_ref, rank_ref, cnt_ref, f0_ref, e1_ref, o_ref, cd):
    tm = hf_ref.shape[0]
    te = u_ref.shape[0]
    sub = PEER_SUB_ROWS

    @pl.when(pl.program_id(1) == 0)
    def _():
        o_ref[...] = jnp.zeros(o_ref.shape, f32)

    a_t = _dot_nt(u_ref[...], hf_ref[...])
    for i1 in range(te // PEER_NK):
        for c in range(0, tm, 128):
            lanes = slice(c, c + 128)
            cnt_b = [jnp.broadcast_to(cnt_ref[h, i1:i1 + 1, lanes], (sub, 128)).astype(bf16) for h in range(PEER_H)]
            f0_b = [jnp.broadcast_to(f0_ref[h, i1:i1 + 1, lanes], (sub, 128)).astype(bf16) for h in range(PEER_H)]
            for r in range(0, PEER_NK, sub):
                wb = None
                for h in range(PEER_H):
                    keep = rank_ref[h, r:r + sub, lanes] < cnt_b[h]
                    term = jnp.where(keep, e1_ref[h, r:r + sub, lanes], jnp.zeros((), bf16)) * f0_b[h]
                    wb = term if wb is None else wb + term
                rows = slice(i1 * PEER_NK + r, i1 * PEER_NK + r + sub)
                cd[rows, lanes] = wb * _gelu(a_t[rows, lanes]).astype(bf16)
    o_ref[...] += lax.dot_general(cd[...], v_ref[...], (((0,), (0,)), ((), ())), preferred_element_type=f32)


def _peer_dense(hf, u_b, v_b, rank1, cnt, f0, e1, tm, te):
    t = hf.shape[0]
    ni1 = te // PEER_NK
    assert ni1 == 8, "one 8-row block of side-0 rows per expert tile"
    rows8 = pl.BlockSpec((PEER_H, ni1, tm), lambda i, j: (0, j, i))
    full = pl.BlockSpec((PEER_H, PEER_NK, tm), lambda i, j: (0, 0, i))
    return pl.pallas_call(
        _peer_dense_kernel,
        grid=(t // tm, PEER_E // te),
        in_specs=[pl.BlockSpec((tm, D), lambda i, j: (i, 0)),
                  pl.BlockSpec((te, D), lambda i, j: (j, 0)),
                  pl.BlockSpec((te, D), lambda i, j: (j, 0)),
                  full, rows8, rows8, full],
        out_specs=pl.BlockSpec((tm, D), lambda i, j: (i, 0)),
        out_shape=jax.ShapeDtypeStruct((t, D), f32),
        scratch_shapes=[pltpu.VMEM((te, tm), bf16)],
        compiler_params=_cp(("parallel", "arbitrary")),
        name="peer_dense",
    )(hf, u_b, v_b, rank1, cnt, f0, e1)


def _final_kernel(x1_ref, p_ref, g2_ref, nw_ref, o_ref):
    x2 = x1_ref[...] + g2_ref[...] * p_ref[...]
    o_ref[...] = x2 * lax.rsqrt(jnp.mean(x2 * x2, axis=-1, keepdims=True) + EPS) * nw_ref[...]


def _final(x1, peer_out, mod, norm_w, per_token, rows_per_seq, row_off):
    t = x1.shape[0]
    tm = min(512, t)
    off = row_off // tm
    return pl.pallas_call(
        _final_kernel,
        grid=(t // tm,),
        in_specs=[pl.BlockSpec((tm, D), lambda i: (i, 0)),
                  pl.BlockSpec((tm, D), lambda i: (i + off, 0)),
                  _mod_specs(per_token, tm, rows_per_seq, 5, 1),
                  pl.BlockSpec((1, D), lambda i: (0, 0))],
        out_specs=pl.BlockSpec((tm, D), lambda i: (i, 0)),
        out_shape=jax.ShapeDtypeStruct((t, D), f32),
        compiler_params=_cp(("parallel",)),
        name="final_norm",
    )(x1, peer_out, mod, norm_w.reshape(1, D))


def _pad_row(v, lo):
    return jnp.zeros((1, 128), f32).at[0, lo:lo + v.shape[0]].set(v.astype(f32))


def _prep_weights(w_in, ssd_conv_w, ssd_conv_b, ssd_dt_bias, ssd_A_log, ssd_D, ssd_norm_w, gdn_conv_w,
                  gdn_dt_bias, gdn_A_log, gdn_norm_w):
    o_z, o_xbc = 0, SSD_INNER
    o_dt = o_xbc + SSD_CONV
    o_qkv = o_dt + SSD_H
    o_a = o_qkv + GDN_CONV
    o_b = o_a + GDN_H
    o_gate = o_b + GDN_H
    cols = [w_in[:, o_z:o_z + SSD_INNER], w_in[:, o_xbc:o_xbc + SSD_INNER], w_in[:, o_gate:o_gate + GDN_V],
            w_in[:, o_qkv:o_qkv + GDN_CONV], w_in[:, o_xbc + SSD_INNER:o_xbc + SSD_CONV],
            w_in[:, o_dt:o_dt + SSD_H], w_in[:, o_a:o_a + GDN_H], w_in[:, o_b:o_b + GDN_H]]
    w_re = jnp.concatenate(cols, axis=1)
    w_re = jnp.pad(w_re, ((0, 0), (0, P_COLS - w_re.shape[1]))).astype(bf16)
    expand = (jnp.arange(128)[:, None] == (jnp.arange(SSD_INNER)[None, :] // SSD_P)).astype(f32)
    return dict(
        w_in_re=w_re,
        ssd_wxs=ssd_conv_w[:, :SSD_INNER], ssd_wbc=ssd_conv_w[:, SSD_INNER:],
        ssd_bxs=ssd_conv_b[None, :SSD_INNER], ssd_bbc=ssd_conv_b[None, SSD_INNER:],
        ssd_cw=ssd_conv_w, ssd_cb=ssd_conv_b[None, :],
        ssd_dtb=_pad_row(ssd_dt_bias, SM_DT), ssd_alog=_pad_row(ssd_A_log, SM_DT),
        ssd_de=jnp.repeat(ssd_D.astype(f32), SSD_P)[None, :], ssd_nw=ssd_norm_w[None, :].astype(f32),
        ssd_expand=expand,
        gdn_cw=gdn_conv_w, gdn_dtb=_pad_row(gdn_dt_bias, SM_A), gdn_alog=_pad_row(gdn_A_log, SM_A),
        gdn_nw=gdn_norm_w[None, :].astype(f32),
    )


PEER_TM = 512
PEER_TE = 1024
PEER_SUB_ROWS = 16


def kernel(x_prompt, x_sample, c_prompt, c_sample, state_ssd, state_ssd_conv, state_gdn, state_gdn_conv, w_ada, b_ada, norm1_w, norm2_w, w_in, ssd_conv_w, ssd_conv_b, ssd_dt_bias, ssd_A_log, ssd_D, ssd_norm_w, gdn_conv_w, gdn_dt_bias, gdn_A_log, gdn_norm_w, w_out, peer_w_q, peer_sub_keys, peer_u, peer_v, final_norm_w):
    assert w_ada.shape[0] == 1, "single layer"
    nb, seq, _ = x_prompt.shape
    ns = x_sample.shape[0]
    assert x_sample.shape[1] == 1 and seq % SSD_CHUNK == 0 and seq % 512 == 0 and ns % 8 == 0
    w = _prep_weights(w_in[0], ssd_conv_w[0], ssd_conv_b[0], ssd_dt_bias[0], ssd_A_log[0], ssd_D[0], ssd_norm_w[0],
                      gdn_conv_w[0], gdn_dt_bias[0], gdn_A_log[0], gdn_norm_w[0])
    w_out_b = w_out[0].astype(bf16)
    wq_t = peer_w_q[0].T.astype(bf16)
    sk = peer_sub_keys[0].reshape(2 * PEER_H, PEER_NK, PEER_DK // 2).astype(bf16)
    u_b = peer_u[0].astype(bf16)
    v_b = peer_v[0].astype(bf16)

    nbp = -(-nb // 8) * 8
    c_all = jnp.concatenate([jnp.pad(c_prompt, ((0, nbp - nb), (0, 0))), c_sample], axis=0)
    mod = _modulation(c_all, w_ada[0], b_ada[0])
    mod_p = mod[:nb].reshape(nb, 1, 6 * D)
    mod_s = mod[nbp:]

    xp = x_prompt.reshape(nb * seq, D)
    xs = x_sample.reshape(ns, D)

    proj_p = _inproj(xp, mod_p, norm1_w[0], w["w_in_re"], False, seq)
    y_ssd_p, ssd_state_p = _ssd_prompt(proj_p, nb, seq, w)
    o_gdn_p, gdn_state_p = _gdn_prompt(proj_p, nb, seq, w)
    x1_p, hf_p = _outproj(y_ssd_p, o_gdn_p, xp, mod_p, norm2_w[0], w_out_b, False, seq)
    tail = proj_p.reshape(nb, seq, P_COLS)[:, seq - (CONV_W - 1):, :]
    ssd_conv_p = jnp.concatenate([tail[..., OFF_XS:OFF_XS + SSD_INNER], tail[..., OFF_BC:OFF_BC + SSD_BC]], axis=-1)
    gdn_conv_p = tail[..., OFF_QKV:OFF_QKV + GDN_CONV]

    proj_s = _inproj(xs, mod_s, norm1_w[0], w["w_in_re"], True, 1)
    y_ssd_s, ssd_conv_s, ssd_state_s = _ssd_step(proj_s, state_ssd_conv[0].reshape(ns, 3 * SSD_CONV), state_ssd[0], w)
    o_gdn_s, gdn_conv_s, gdn_state_s = _gdn_step(proj_s, state_gdn_conv[0].reshape(ns, 3 * GDN_CONV), state_gdn[0], w)
    x1_s, hf_s = _outproj(y_ssd_s, o_gdn_s, xs, mod_s, norm2_w[0], w_out_b, True, 1)

    tm = PEER_TM
    n_tok = nb * seq + ns
    pad = -n_tok % tm
    hf = jnp.concatenate([hf_p, hf_s, jnp.zeros((pad, D), bf16)], axis=0)
    scores_t = _peer_scores(hf, wq_t, sk, tm)
    rank1, cnt, f0, e1 = _peer_select(scores_t, 128)
    peer_out = _peer_dense(hf, u_b, v_b, rank1, cnt, f0, e1, tm, PEER_TE)

    y_p = _final(x1_p, peer_out, mod_p, final_norm_w, False, seq, 0)
    y_s = _final(x1_s, peer_out, mod_s, final_norm_w, True, 1, nb * seq)

    return (y_p.reshape(nb, seq, D), y_s.reshape(ns, 1, D),
            ssd_state_p[None], ssd_conv_p[None], gdn_state_p[None], gdn_conv_p[None],
            ssd_state_s[None], ssd_conv_s.reshape(1, ns, CONV_W - 1, SSD_CONV),
            gdn_state_s[None], gdn_conv_s.reshape(1, ns, CONV_W - 1, GDN_CONV))
```

```python
import functools
import math

import jax
import jax.numpy as jnp
from jax import lax
from jax.experimental import pallas as pl
from jax.experimental.pallas import tpu as pltpu

f32 = jnp.float32
bf16 = jnp.bfloat16

D = 2048
CONV_W = 4
SSD_P = 64
SSD_INNER = D // 2
SSD_H = SSD_INNER // SSD_P
SSD_G = 2
SSD_N = 128
SSD_CHUNK = 128
SSD_BC = 2 * SSD_G * SSD_N
SSD_CONV = SSD_INNER + SSD_BC
GDN_DK = 128
GDN_DV = 128
GDN_V = D // 2
GDN_H = GDN_V // GDN_DV
GDN_CHUNK = 64
GDN_CONV = GDN_H * (2 * GDN_DK + GDN_DV)
PEER_H = 8
PEER_NK = 128
PEER_E = PEER_NK * PEER_NK
PEER_DK = 256
PEER_K = 16
EPS = 1e-6
NEG = -1e30

OFF_Z, OFF_XS, OFF_GATE, OFF_QKV, OFF_BC, OFF_SM = 0, 1024, 2048, 3072, 6144, 6656
P_COLS = 7168
SM_DT, SM_A, SM_B = 0, 16, 24

VMEM_LIMIT = 56 * 1024 * 1024


def _cp(sem):
    return pltpu.CompilerParams(dimension_semantics=sem, vmem_limit_bytes=VMEM_LIMIT)


def _silu(x):
    return x * jax.nn.sigmoid(x)


def _softplus(x):
    return jnp.maximum(x, 0.0) + jnp.log1p(jnp.exp(-jnp.abs(x)))


def _dot(a, b):
    return jnp.dot(a, b, preferred_element_type=f32)


def _dot_nt(a, b):
    return lax.dot_general(a, b, (((1,), (1,)), ((), ())), preferred_element_type=f32)


def _dot_hi(a, b):
    return jnp.dot(a, b, preferred_element_type=f32, precision=lax.Precision.HIGHEST)


def _iota(shape, dim):
    return lax.broadcasted_iota(jnp.int32, shape, dim)


def _mod_kernel(c_ref, w_ref, b_ref, o_ref):
    a = _silu(c_ref[...]).astype(bf16)
    o_ref[...] = _dot(a, w_ref[...].astype(bf16)) + b_ref[...]


def _modulation(c_all, w_ada, b_ada):
    m, tn = c_all.shape[0], 1024
    return pl.pallas_call(
        _mod_kernel,
        grid=(6 * D // tn,),
        in_specs=[pl.BlockSpec((m, D), lambda j: (0, 0)),
                  pl.BlockSpec((D, tn), lambda j: (0, j)),
                  pl.BlockSpec((1, tn), lambda j: (0, j))],
        out_specs=pl.BlockSpec((m, tn), lambda j: (0, j)),
        out_shape=jax.ShapeDtypeStruct((m, 6 * D), f32),
        compiler_params=_cp(("arbitrary",)),
        name="adaln_mod",
    )(c_all, w_ada, b_ada.reshape(1, 6 * D))


def _mod_specs(per_token, tm, rows_per_seq, which, grid_rank):
    if per_token:
        if grid_rank == 2:
            return pl.BlockSpec((tm, D), lambda i, j: (i, which))
        return pl.BlockSpec((tm, D), lambda i: (i, which))
    tps = rows_per_seq // tm
    if grid_rank == 2:
        return pl.BlockSpec((None, 1, D), lambda i, j: (i // tps, 0, which))
    return pl.BlockSpec((None, 1, D), lambda i: (i // tps, 0, which))


def _inproj_kernel(x_ref, sh_ref, sc_ref, nw_ref, w_ref, o_ref, hm_ref):
    @pl.when(pl.program_id(1) == 0)
    def _():
        x = x_ref[...]
        y = x * lax.rsqrt(jnp.mean(x * x, axis=-1, keepdims=True) + EPS) * nw_ref[...]
        hm_ref[...] = (y * (1.0 + sc_ref[...]) + sh_ref[...]).astype(bf16)

    o_ref[...] = _dot(hm_ref[...], w_ref[...])


def _inproj(x, mod, norm_w, w_re, per_token, rows_per_seq):
    t = x.shape[0]
    tm = min(1024, t)
    assert per_token or rows_per_seq % tm == 0
    tn = 1024
    return pl.pallas_call(
        _inproj_kernel,
        grid=(t // tm, P_COLS // tn),
        in_specs=[pl.BlockSpec((tm, D), lambda i, j: (i, 0)),
                  _mod_specs(per_token, tm, rows_per_seq, 0, 2),
                  _mod_specs(per_token, tm, rows_per_seq, 1, 2),
                  pl.BlockSpec((1, D), lambda i, j: (0, 0)),
                  pl.BlockSpec((D, tn), lambda i, j: (0, j))],
        out_specs=pl.BlockSpec((tm, tn), lambda i, j: (i, j)),
        out_shape=jax.ShapeDtypeStruct((t, P_COLS), f32),
        scratch_shapes=[pltpu.VMEM((tm, D), bf16)],
        compiler_params=_cp(("parallel", "arbitrary")),
        name="inproj",
    )(x, mod, mod, norm_w.reshape(1, D), w_re)


def _conv_chunk(ext_ref, cur, w_ref, b_row, q):
    ext_ref[8:8 + q, :] = cur
    acc = w_ref[CONV_W - 1:CONV_W, :] * cur
    for k in range(CONV_W - 1):
        acc = acc + w_ref[k:k + 1, :] * ext_ref[5 + k:5 + k + q, :]
    ext_ref[0:8, :] = cur[q - 8:q, :]
    if b_row is not None:
        acc = acc + b_row
    return acc


def _group_rmsnorm(y, nw, width):
    outs = []
    for g in range(y.shape[-1] // width):
        seg = y[:, g * width:(g + 1) * width]
        ms = jnp.mean(seg * seg, axis=-1, keepdims=True)
        outs.append(seg * lax.rsqrt(ms + EPS) * nw[:, g * width:(g + 1) * width])
    return jnp.concatenate(outs, axis=-1)


def _lane_mask(shape, lo, hi):
    lane = _iota(shape, len(shape) - 1)
    return (lane >= lo) & (lane < hi)


def _ssd_kernel(z_ref, xs_ref, bc_ref, sm_ref, wxs_ref, wbc_ref, bxs_ref, bbc_ref, dtb_ref, alog_ref, de_ref,
                nw_ref, y_ref, st_ref, extx, extb, ht, yscr):
    q = SSD_CHUNK
    c = pl.program_id(1)

    @pl.when(c == 0)
    def _():
        extx[0:8, :] = jnp.zeros((8, SSD_INNER), f32)
        extb[0:8, :] = jnp.zeros((8, SSD_BC), f32)
        ht[...] = jnp.zeros(ht.shape, f32)

    xs = _silu(_conv_chunk(extx, xs_ref[...], wxs_ref, bxs_ref[...], q))
    bcv = _silu(_conv_chunk(extb, bc_ref[...], wbc_ref, bbc_ref[...], q))

    head_lanes = _lane_mask((q, 128), SM_DT, SM_DT + SSD_H)
    dt = jnp.where(head_lanes, _softplus(sm_ref[...] + dtb_ref[...]), 0.0)
    a_row = -jnp.exp(alog_ref[...])
    tri = (_iota((q, q), 0) >= _iota((q, q), 1))
    cs = _dot_hi(tri.astype(f32), dt * a_row)
    cs_t = cs.T
    dt_t = dt.T
    last_col = cs_t[:, q - 1:q]
    w_t = dt_t * jnp.exp(last_col - cs_t)
    ecs = jnp.exp(cs)
    lane_lo = _iota((q, 128), 1) < SSD_P

    for g in range(SSD_G):
        b_g = bcv[:, g * SSD_N:(g + 1) * SSD_N]
        c_g = bcv[:, SSD_G * SSD_N + g * SSD_N:SSD_G * SSD_N + (g + 1) * SSD_N]
        b_gt = b_g.T
        cb = _dot(c_g.astype(bf16), b_gt.astype(bf16))
        for r2 in range(SSD_H // SSD_G // 2):
            pair = g * (SSD_H // SSD_G // 2) + r2
            xs_pair = xs[:, pair * 128:(pair + 1) * 128]
            xs_pair_b = xs_pair.astype(bf16)
            h_prev = ht[pair]
            rhs = jnp.concatenate([xs_pair_b, h_prev.astype(bf16)], axis=0)
            ys, hs, els = [], [], []
            for e in range(2):
                h = 2 * pair + e
                diff = cs[:, h:h + 1] - cs_t[h:h + 1, :]
                dec = jnp.exp(jnp.where(tri, diff, NEG))
                m = cb * dec * dt_t[h:h + 1, :]
                lhs = jnp.concatenate([m, c_g * ecs[:, h:h + 1]], axis=1).astype(bf16)
                ys.append(_dot(lhs, rhs))
                bw = (b_gt * w_t[h:h + 1, :]).astype(bf16)
                hs.append(_dot(bw, xs_pair_b))
                els.append(jnp.exp(last_col[h:h + 1, :]))
            yscr[:, pair * 128:(pair + 1) * 128] = jnp.where(lane_lo, ys[0], ys[1])
            ht[pair] = h_prev * jnp.where(lane_lo, els[0], els[1]) + jnp.where(lane_lo, hs[0], hs[1])

    z = z_ref[...]
    y = (yscr[...] + de_ref[...] * xs) * _silu(z)
    y_ref[...] = _group_rmsnorm(y, nw_ref[...], SSD_INNER // SSD_G).astype(y_ref.dtype)

    @pl.when(c == pl.num_programs(1) - 1)
    def _():
        for pair in range(SSD_H // 2):
            t = ht[pair].T
            st_ref[2 * pair] = t[0:SSD_P]
            st_ref[2 * pair + 1] = t[SSD_P:2 * SSD_P]


def _ssd_prompt(proj, nb, seq, w):
    nc = seq // SSD_CHUNK
    q = SSD_CHUNK
    row = lambda n, c: n * nc + c
    const = lambda shape: pl.BlockSpec(shape, lambda n, c: (0, 0))
    return pl.pallas_call(
        _ssd_kernel,
        grid=(nb, nc),
        in_specs=[pl.BlockSpec((q, 1024), lambda n, c: (row(n, c), OFF_Z // 1024)),
                  pl.BlockSpec((q, 1024), lambda n, c: (row(n, c), OFF_XS // 1024)),
                  pl.BlockSpec((q, SSD_BC), lambda n, c: (row(n, c), OFF_BC // SSD_BC)),
                  pl.BlockSpec((q, 128), lambda n, c: (row(n, c), OFF_SM // 128)),
                  const((CONV_W, SSD_INNER)), const((CONV_W, SSD_BC)), const((1, SSD_INNER)), const((1, SSD_BC)),
                  const((1, 128)), const((1, 128)), const((1, SSD_INNER)), const((1, SSD_INNER))],
        out_specs=[pl.BlockSpec((q, SSD_INNER), lambda n, c: (row(n, c), 0)),
                   pl.BlockSpec((None, SSD_H, SSD_P, SSD_N), lambda n, c: (n, 0, 0, 0))],
        out_shape=[jax.ShapeDtypeStruct((nb * seq, SSD_INNER), bf16),
                   jax.ShapeDtypeStruct((nb, SSD_H, SSD_P, SSD_N), f32)],
        scratch_shapes=[pltpu.VMEM((q + 8, SSD_INNER), f32), pltpu.VMEM((q + 8, SSD_BC), f32),
                        pltpu.VMEM((SSD_H // 2, SSD_N, 2 * SSD_P), f32), pltpu.VMEM((q, SSD_INNER), f32)],
        compiler_params=_cp(("parallel", "arbitrary")),
        name="ssd_scan",
    )(proj, proj, proj, proj, w["ssd_wxs"], w["ssd_wbc"], w["ssd_bxs"], w["ssd_bbc"], w["ssd_dtb"],
      w["ssd_alog"], w["ssd_de"], w["ssd_nw"])


def _gdn_kernel(qkv_ref, gate_ref, sm_ref, cw_ref, dtb_ref, alog_ref, nw_ref, o_ref, st_ref, ext, s_scr):
    q = GDN_CHUNK
    c = pl.program_id(1)

    @pl.when(c == 0)
    def _():
        ext[0:8, :] = jnp.zeros((8, GDN_CONV), f32)
        s_scr[...] = jnp.zeros(s_scr.shape, f32)

    act = _silu(_conv_chunk(ext, qkv_ref[...], cw_ref, None, q))
    sm = sm_ref[...]
    a_lanes = _lane_mask((q, 128), SM_A, SM_A + GDN_H)
    g = jnp.where(a_lanes, -jnp.exp(alog_ref[...]) * _softplus(sm + dtb_ref[...]), 0.0)
    beta = jax.nn.sigmoid(sm)
    incl = _iota((q, q), 0) >= _iota((q, q), 1)
    strict = _iota((q, q), 0) > _iota((q, q), 1)
    gc = _dot_hi(incl.astype(f32), g)
    gc_t = gc.T
    egc = jnp.exp(gc)
    gate = gate_ref[...]
    nw = nw_ref[...]

    heads = range(GDN_H)
    rb, cb_ = _iota((q, q), 0), _iota((q, q), 1)
    same16 = (rb // 16) == (cb_ // 16)
    same32 = (rb // 32) == (cb_ // 32)
    diag_mask = strict & same16
    off32_mask = strict & same32 & jnp.logical_not(same16)
    off64_mask = strict & jnp.logical_not(same32)

    qn, kn, kb, dec, ecol, bcol, gcol, vh = [], [], [], [], [], [], [], []
    for h in heads:
        qh = act[:, h * GDN_DK:(h + 1) * GDN_DK]
        kh = act[:, GDN_H * GDN_DK + h * GDN_DK:GDN_H * GDN_DK + (h + 1) * GDN_DK]
        vh.append(act[:, 2 * GDN_H * GDN_DK + h * GDN_DV:2 * GDN_H * GDN_DK + (h + 1) * GDN_DV])
        qn.append(qh * lax.rsqrt(jnp.sum(qh * qh, axis=-1, keepdims=True) + EPS) * (GDN_DK ** -0.5))
        kn.append(kh * lax.rsqrt(jnp.sum(kh * kh, axis=-1, keepdims=True) + EPS))
        gcol.append(gc[:, SM_A + h:SM_A + h + 1])
        grow = gc_t[SM_A + h:SM_A + h + 1, :]
        ecol.append(egc[:, SM_A + h:SM_A + h + 1])
        bcol.append(beta[:, SM_B + h:SM_B + h + 1])
        dec.append(jnp.exp(jnp.where(incl, gcol[h] - grow, NEG)))
        kb.append(kn[h] * bcol[h])
    kn_b = [kn[h].astype(bf16) for h in heads]
    lmat = [_dot_nt(kb[h].astype(bf16), kn_b[h]) * dec[h] for h in heads]
    qk = [_dot_nt(qn[h].astype(bf16), kn_b[h]) * dec[h] for h in heads]

    dotb = lambda a, b: _dot(a.astype(bf16), b.astype(bf16))
    n_pow = [jnp.where(diag_mask, -lmat[h], 0.0) for h in heads]
    t_m = list(n_pow)
    for _ in range(3):
        n_pow = [dotb(n_pow[h], n_pow[h]) for h in heads]
        t_m = [t_m[h] + n_pow[h] + dotb(t_m[h], n_pow[h]) for h in heads]
    for mask in (off32_mask, off64_mask):
        off = [jnp.where(mask, lmat[h], 0.0) for h in heads]
        y = [off[h] + dotb(off[h], t_m[h]) for h in heads]
        t_m = [t_m[h] - (y[h] + dotb(t_m[h], y[h])) for h in heads]

    rhs = [jnp.concatenate([vh[h] * bcol[h], kb[h] * ecol[h]], axis=1) for h in heads]
    sol = [rhs[h] + dotb(t_m[h], rhs[h]) for h in heads]
    s_prev = [s_scr[h] for h in heads]
    s_b = [s_prev[h].astype(bf16) for h in heads]
    v_new = [sol[h][:, :GDN_DV] - _dot(sol[h][:, GDN_DV:].astype(bf16), s_b[h]) for h in heads]
    v_new_b = [v_new[h].astype(bf16) for h in heads]
    o = [_dot((qn[h] * ecol[h]).astype(bf16), s_b[h]) + _dot(qk[h].astype(bf16), v_new_b[h]) for h in heads]
    for h in heads:
        glast = gcol[h][q - 1:q, :]
        kdec = kn[h] * jnp.exp(glast - gcol[h])
        s_scr[h] = s_prev[h] * jnp.exp(glast) + _dot(kdec.T.astype(bf16), v_new_b[h])
    for h in heads:
        on = o[h] * lax.rsqrt(jnp.mean(o[h] * o[h], axis=-1, keepdims=True) + EPS) * nw
        o_ref[:, h * GDN_DV:(h + 1) * GDN_DV] = (on * _silu(gate[:, h * GDN_DV:(h + 1) * GDN_DV])).astype(o_ref.dtype)

    @pl.when(c == pl.num_programs(1) - 1)
    def _():
        st_ref[...] = s_scr[...]


def _gdn_prompt(proj, nb, seq, w):
    nc = seq // GDN_CHUNK
    q = GDN_CHUNK
    row = lambda n, c: n * nc + c
    const = lambda shape: pl.BlockSpec(shape, lambda n, c: (0, 0))
    return pl.pallas_call(
        _gdn_kernel,
        grid=(nb, nc),
        in_specs=[pl.BlockSpec((q, GDN_CONV), lambda n, c: (row(n, c), OFF_QKV // GDN_CONV)),
                  pl.BlockSpec((q, GDN_V), lambda n, c: (row(n, c), OFF_GATE // GDN_V)),
                  pl.BlockSpec((q, 128), lambda n, c: (row(n, c), OFF_SM // 128)),
                  const((CONV_W, GDN_CONV)), const((1, 128)), const((1, 128)), const((1, GDN_DV))],
        out_specs=[pl.BlockSpec((q, GDN_V), lambda n, c: (row(n, c), 0)),
                   pl.BlockSpec((None, GDN_H, GDN_DK, GDN_DV), lambda n, c: (n, 0, 0, 0))],
        out_shape=[jax.ShapeDtypeStruct((nb * seq, GDN_V), bf16),
                   jax.ShapeDtypeStruct((nb, GDN_H, GDN_DK, GDN_DV), f32)],
        scratch_shapes=[pltpu.VMEM((q + 8, GDN_CONV), f32), pltpu.VMEM((GDN_H, GDN_DK, GDN_DV), f32)],
        compiler_params=_cp(("parallel", "arbitrary")),
        name="gdn_scan",
    )(proj, proj, proj, w["gdn_cw"], w["gdn_dtb"], w["gdn_alog"], w["gdn_nw"])


def _conv_step(buf, xraw, w_ref, width):
    acc = w_ref[CONV_W - 1:CONV_W, :] * xraw
    for k in range(CONV_W - 1):
        acc = acc + w_ref[k:k + 1, :] * buf[:, k * width:(k + 1) * width]
    return acc, jnp.concatenate([buf[:, width:], xraw], axis=1)


def _ssd_step_kernel(z_ref, xs_ref, bc_ref, sm_ref, buf_ref, st_ref, cw_ref, cb_ref, dtb_ref, alog_ref, de_ref,
                     nw_ref, ex_ref, y_ref, nbuf_ref, nst_ref, yscr):
    bs = z_ref.shape[0]
    xraw = jnp.concatenate([xs_ref[...], bc_ref[...]], axis=1)
    conv, nbuf = _conv_step(buf_ref[...], xraw, cw_ref, SSD_CONV)
    nbuf_ref[...] = nbuf
    act = _silu(conv + cb_ref[...])
    xs = act[:, :SSD_INNER]
    head_lanes = _lane_mask((bs, 128), SM_DT, SM_DT + SSD_H)
    dt = jnp.where(head_lanes, _softplus(sm_ref[...] + dtb_ref[...]), 0.0)
    da = jnp.exp(dt * (-jnp.exp(alog_ref[...])))
    dt_e = _dot_hi(dt, ex_ref[...])
    da_e = _dot_hi(da, ex_ref[...])
    cols = jnp.concatenate([xs * dt_e, da_e], axis=0).T
    gw = SSD_INNER // SSD_G
    hg = SSD_H // SSD_G
    for b in range(bs):
        xcol = cols[:, b:b + 1]
        dcol = cols[:, bs + b:bs + b + 1]
        for g in range(SSD_G):
            b_row = act[b:b + 1, SSD_INNER + g * SSD_N:SSD_INNER + (g + 1) * SSD_N]
            c_row = act[b:b + 1, SSD_INNER + (SSD_G + g) * SSD_N:SSD_INNER + (SSD_G + g + 1) * SSD_N]
            st = st_ref[b, g * hg:(g + 1) * hg].reshape(gw, SSD_N)
            new = st * dcol[g * gw:(g + 1) * gw] + xcol[g * gw:(g + 1) * gw] * b_row
            nst_ref[b, g * hg:(g + 1) * hg] = new.reshape(hg, SSD_P, SSD_N)
            c8 = jnp.broadcast_to(c_row, (8, SSD_N))
            yscr[b:b + 1, g * gw:(g + 1) * gw] = _dot_nt(c8, new)[0:1]
    y = (yscr[...] + de_ref[...] * xs) * _silu(z_ref[...])
    y_ref[...] = _group_rmsnorm(y, nw_ref[...], gw).astype(y_ref.dtype)


def _ssd_step(proj, buf, state, w, bs=8):
    n = proj.shape[0]
    const = lambda shape: pl.BlockSpec(shape, lambda i: (0,) * len(shape))
    return pl.pallas_call(
        _ssd_step_kernel,
        grid=(n // bs,),
        in_specs=[pl.BlockSpec((bs, 1024), lambda i: (i, OFF_Z // 1024)),
                  pl.BlockSpec((bs, 1024), lambda i: (i, OFF_XS // 1024)),
                  pl.BlockSpec((bs, SSD_BC), lambda i: (i, OFF_BC // SSD_BC)),
                  pl.BlockSpec((bs, 128), lambda i: (i, OFF_SM // 128)),
                  pl.BlockSpec((bs, 3 * SSD_CONV), lambda i: (i, 0)),
                  pl.BlockSpec((bs, SSD_H, SSD_P, SSD_N), lambda i: (i, 0, 0, 0)),
                  const((CONV_W, SSD_CONV)), const((1, SSD_CONV)), const((1, 128)), const((1, 128)),
                  const((1, SSD_INNER)), const((1, SSD_INNER)), const((128, SSD_INNER))],
        out_specs=[pl.BlockSpec((bs, SSD_INNER), lambda i: (i, 0)),
                   pl.BlockSpec((bs, 3 * SSD_CONV), lambda i: (i, 0)),
                   pl.BlockSpec((bs, SSD_H, SSD_P, SSD_N), lambda i: (i, 0, 0, 0))],
        out_shape=[jax.ShapeDtypeStruct((n, SSD_INNER), bf16),
                   jax.ShapeDtypeStruct((n, 3 * SSD_CONV), f32),
                   jax.ShapeDtypeStruct((n, SSD_H, SSD_P, SSD_N), f32)],
        scratch_shapes=[pltpu.VMEM((bs, SSD_INNER), f32)],
        compiler_params=_cp(("parallel",)),
        name="ssd_step",
    )(proj, proj, proj, proj, buf, state, w["ssd_cw"], w["ssd_cb"], w["ssd_dtb"], w["ssd_alog"], w["ssd_de"],
      w["ssd_nw"], w["ssd_expand"])


def _gdn_step_kernel(qkv_ref, gate_ref, sm_ref, buf_ref, st_ref, cw_ref, dtb_ref, alog_ref, nw_ref,
                     o_ref, nbuf_ref, nst_ref, oscr):
    bs = qkv_ref.shape[0]
    conv, nbuf = _conv_step(buf_ref[...], qkv_ref[...], cw_ref, GDN_CONV)
    nbuf_ref[...] = nbuf
    act = _silu(conv)
    sm = sm_ref[...]
    eg = jnp.exp(-jnp.exp(alog_ref[...]) * _softplus(sm + dtb_ref[...]))
    beta = jax.nn.sigmoid(sm)
    qs, ks = [], []
    for h in range(GDN_H):
        qh = act[:, h * GDN_DK:(h + 1) * GDN_DK]
        kh = act[:, GDN_H * GDN_DK + h * GDN_DK:GDN_H * GDN_DK + (h + 1) * GDN_DK]
        qs.append(qh * lax.rsqrt(jnp.sum(qh * qh, axis=-1, keepdims=True) + EPS) * (GDN_DK ** -0.5))
        ks.append(kh * lax.rsqrt(jnp.sum(kh * kh, axis=-1, keepdims=True) + EPS))
    qk_t = jnp.concatenate(qs + ks, axis=1).T
    for b in range(bs):
        for h in range(GDN_H):
            qcol = qk_t[h * GDN_DK:(h + 1) * GDN_DK, b:b + 1]
            kcol = qk_t[(GDN_H + h) * GDN_DK:(GDN_H + h + 1) * GDN_DK, b:b + 1]
            vrow = act[b:b + 1, 2 * GDN_H * GDN_DK + h * GDN_DV:2 * GDN_H * GDN_DK + (h + 1) * GDN_DV]
            egs = eg[b:b + 1, SM_A + h:SM_A + h + 1]
            bet = beta[b:b + 1, SM_B + h:SM_B + h + 1]
            s = st_ref[b, h]
            v_new = bet * (vrow - egs * jnp.sum(kcol * s, axis=0, keepdims=True))
            s_new = s * egs + kcol * v_new
            nst_ref[b, h] = s_new
            oscr[b:b + 1, h * GDN_DV:(h + 1) * GDN_DV] = jnp.sum(qcol * s_new, axis=0, keepdims=True)
    o = oscr[...]
    nw = nw_ref[...]
    gate = gate_ref[...]
    for h in range(GDN_H):
        oh = o[:, h * GDN_DV:(h + 1) * GDN_DV]
        on = oh * lax.rsqrt(jnp.mean(oh * oh, axis=-1, keepdims=True) + EPS) * nw
        o_ref[:, h * GDN_DV:(h + 1) * GDN_DV] = (on * _silu(gate[:, h * GDN_DV:(h + 1) * GDN_DV])).astype(o_ref.dtype)


def _gdn_step(proj, buf, state, w, bs=8):
    n = proj.shape[0]
    const = lambda shape: pl.BlockSpec(shape, lambda i: (0,) * len(shape))
    return pl.pallas_call(
        _gdn_step_kernel,
        grid=(n // bs,),
        in_specs=[pl.BlockSpec((bs, GDN_CONV), lambda i: (i, OFF_QKV // GDN_CONV)),
                  pl.BlockSpec((bs, GDN_V), lambda i: (i, OFF_GATE // GDN_V)),
                  pl.BlockSpec((bs, 128), lambda i: (i, OFF_SM // 128)),
                  pl.BlockSpec((bs, 3 * GDN_CONV), lambda i: (i, 0)),
                  pl.BlockSpec((bs, GDN_H, GDN_DK, GDN_DV), lambda i: (i, 0, 0, 0)),
                  const((CONV_W, GDN_CONV)), const((1, 128)), const((1, 128)), const((1, GDN_DV))],
        out_specs=[pl.BlockSpec((bs, GDN_V), lambda i: (i, 0)),
                   pl.BlockSpec((bs, 3 * GDN_CONV), lambda i: (i, 0)),
                   pl.BlockSpec((bs, GDN_H, GDN_DK, GDN_DV), lambda i: (i, 0, 0, 0))],
        out_shape=[jax.ShapeDtypeStruct((n, GDN_V), bf16),
                   jax.ShapeDtypeStruct((n, 3 * GDN_CONV), f32),
                   jax.ShapeDtypeStruct((n, GDN_H, GDN_DK, GDN_DV), f32)],
        scratch_shapes=[pltpu.VMEM((bs, GDN_V), f32)],
        compiler_params=_cp(("parallel",)),
        name="gdn_step",
    )(proj, proj, proj, buf, state, w["gdn_cw"], w["gdn_dtb"], w["gdn_alog"], w["gdn_nw"])


def _outproj_kernel(ys_ref, og_ref, x_ref, g1_ref, sh_ref, sc_ref, nw_ref, w_ref, x1_ref, hf_ref):
    m = _dot(ys_ref[...], w_ref[0:SSD_INNER, :]) + _dot(og_ref[...], w_ref[SSD_INNER:, :])
    x1 = x_ref[...] + g1_ref[...] * m
    x1_ref[...] = x1
    y = x1 * lax.rsqrt(jnp.mean(x1 * x1, axis=-1, keepdims=True) + EPS) * nw_ref[...]
    hf_ref[...] = (y * (1.0 + sc_ref[...]) + sh_ref[...]).astype(hf_ref.dtype)


def _outproj(ys, og, x, mod, norm_w, w_out_b, per_token, rows_per_seq):
    t = x.shape[0]
    tm = min(512, t)
    return pl.pallas_call(
        _outproj_kernel,
        grid=(t // tm,),
        in_specs=[pl.BlockSpec((tm, SSD_INNER), lambda i: (i, 0)),
                  pl.BlockSpec((tm, GDN_V), lambda i: (i, 0)),
                  pl.BlockSpec((tm, D), lambda i: (i, 0)),
                  _mod_specs(per_token, tm, rows_per_seq, 2, 1),
                  _mod_specs(per_token, tm, rows_per_seq, 3, 1),
                  _mod_specs(per_token, tm, rows_per_seq, 4, 1),
                  pl.BlockSpec((1, D), lambda i: (0, 0)),
                  pl.BlockSpec((D, D), lambda i: (0, 0))],
        out_specs=[pl.BlockSpec((tm, D), lambda i: (i, 0)), pl.BlockSpec((tm, D), lambda i: (i, 0))],
        out_shape=[jax.ShapeDtypeStruct((t, D), f32), jax.ShapeDtypeStruct((t, D), bf16)],
        compiler_params=_cp(("parallel",)),
        name="outproj",
    )(ys, og, x, mod, mod, mod, norm_w.reshape(1, D), w_out_b)


def _pick_token_tile(i, n_main, main_ref, tail_ref, dst_ref):
    @pl.when(i < n_main)
    def _():
        dst_ref[...] = main_ref[...]

    @pl.when(i >= n_main)
    def _():
        dst_ref[...] = tail_ref[...]


def _peer_scores_kernel(hfp_ref, hfs_ref, wq_ref, sk_ref, o_ref, hf_scr):
    _pick_token_tile(pl.program_id(0), pl.num_programs(0) - 1, hfp_ref, hfs_ref, hf_scr)
    q_t = _dot_nt(wq_ref[...], hf_scr[...])
    half = PEER_DK // 2
    for hs in range(2 * PEER_H):
        o_ref[hs] = _dot(sk_ref[hs], q_t[hs * half:(hs + 1) * half].astype(bf16))


def _peer_scores(hf_p, hf_s, wq_b, sk, tm):
    n_main = hf_p.shape[0] // tm
    t = (n_main + 1) * tm
    return pl.pallas_call(
        _peer_scores_kernel,
        grid=(n_main + 1,),
        in_specs=[pl.BlockSpec((tm, D), lambda i: (jnp.minimum(i, n_main - 1), 0)),
                  pl.BlockSpec((tm, D), lambda i: (0, 0)),
                  pl.BlockSpec((PEER_H * PEER_DK, D), lambda i: (0, 0)),
                  pl.BlockSpec((2 * PEER_H, PEER_NK, PEER_DK // 2), lambda i: (0, 0, 0))],
        out_specs=pl.BlockSpec((2 * PEER_H, PEER_NK, tm), lambda i: (0, 0, i)),
        out_shape=jax.ShapeDtypeStruct((2 * PEER_H, PEER_NK, t), f32),
        scratch_shapes=[pltpu.VMEM((tm, D), bf16)],
        compiler_params=_cp(("arbitrary",)),
        name="peer_scores",
    )(hf_p, hf_s, wq_b, sk)


_STAIR = [PEER_K // (k1 + 1) for k1 in range(PEER_K)]
_STAIR_ROWS = -(-sum(_STAIR) // 8) * 8


def _peer_select_kernel(s_ref, rank_ref, cnt_ref, f0_ref, e1_ref, v1_scr, cand):
    tm = s_ref.shape[-1]
    for h in range(PEER_H):
        s0 = s_ref[2 * h]
        s1 = s_ref[2 * h + 1]
        tops = []
        rank1 = jnp.full((PEER_NK, tm), float(PEER_K), f32)
        for side, cur in enumerate((s0, s1)):
            vals = []
            for k in range(PEER_K):
                m = jnp.max(cur, axis=0, keepdims=True)
                vals.append(m)
                hit = cur == m
                if side == 1:
                    rank1 = jnp.where(hit, float(k), rank1)
                cur = jnp.where(hit, -jnp.inf, cur)
            tops.append(vals)
        for k in range(PEER_K):
            v1_scr[k:k + 1, :] = tops[1][k]
        cand[_STAIR_ROWS - 8:_STAIR_ROWS, :] = jnp.full((8, tm), -jnp.inf, f32)
        off = 0
        for k1 in range(PEER_K):
            cand[off:off + _STAIR[k1], :] = tops[0][k1] + v1_scr[0:_STAIR[k1], :]
            off += _STAIR[k1]
        cv = cand[...]
        cur = cv
        for _ in range(PEER_K):
            tau = jnp.max(cur, axis=0, keepdims=True)
            cur = jnp.where(cur == tau, -jnp.inf, cur)
        m0, m1 = tops[0][0], tops[1][0]
        zsum = jnp.sum(jnp.where(cv >= tau, jnp.exp(cv - (m0 + m1)), 0.0), axis=0, keepdims=True)
        cnt = jnp.zeros((PEER_NK, tm), f32)
        for k1 in range(PEER_K):
            v1 = v1_scr[0:_STAIR[k1], :]
            n_k = jnp.sum(jnp.where(tops[0][k1] + v1 >= tau, 1.0, 0.0), axis=0, keepdims=True)
            cnt = jnp.where(s0 == tops[0][k1], n_k, cnt)
        rank_ref[h] = rank1.astype(bf16)
        cnt_ref[h] = cnt
        f0_ref[h] = jnp.exp(s0 - m0) / zsum
        e1_ref[h] = jnp.exp(s1 - m1).astype(bf16)


def _peer_select(scores_t, tm):
    t = scores_t.shape[-1]
    spec = pl.BlockSpec((PEER_H, PEER_NK, tm), lambda i: (0, 0, i))
    shape = lambda dt: jax.ShapeDtypeStruct((PEER_H, PEER_NK, t), dt)
    return pl.pallas_call(
        _peer_select_kernel,
        grid=(t // tm,),
        in_specs=[pl.BlockSpec((2 * PEER_H, PEER_NK, tm), lambda i: (0, 0, i))],
        out_specs=[spec, spec, spec, spec],
        out_shape=[shape(bf16), shape(f32), shape(f32), shape(bf16)],
        scratch_shapes=[pltpu.VMEM((PEER_K, tm), f32), pltpu.VMEM((_STAIR_ROWS, tm), f32)],
        compiler_params=_cp(("parallel",)),
        name="peer_select",
    )(scores_t)


def _gelu(x):
    return 0.5 * x * (1.0 + lax.erf(x * (1.0 / math.sqrt(2.0))))


def _peer_dense_kernel(hfp_ref, hfs_ref, u_ref, v_ref, rank_ref, cnt_ref, f0_ref, e1_ref, o_ref, cd, hf_scr):
    tm = hf_scr.shape[0]
    te = u_ref.shape[0]
    sub = PEER_SUB_ROWS

    @pl.when(pl.program_id(1) == 0)
    def _():
        o_ref[...] = jnp.zeros(o_ref.shape, f32)
        _pick_token_tile(pl.program_id(0), pl.num_programs(0) - 1, hfp_ref, hfs_ref, hf_scr)

    a_t = _dot_nt(u_ref[...], hf_scr[...])
    for i1 in range(te // PEER_NK):
        for c in range(0, tm, 128):
            lanes = slice(c, c + 128)
            cnt_b = [jnp.broadcast_to(cnt_ref[h, i1:i1 + 1, lanes], (sub, 128)).astype(bf16) for h in range(PEER_H)]
            f0_b = [jnp.broadcast_to(f0_ref[h, i1:i1 + 1, lanes], (sub, 128)).astype(bf16) for h in range(PEER_H)]
            for r in range(0, PEER_NK, sub):
                wb = None
                for h in range(PEER_H):
                    keep = rank_ref[h, r:r + sub, lanes] < cnt_b[h]
                    term = jnp.where(keep, e1_ref[h, r:r + sub, lanes], jnp.zeros((), bf16)) * f0_b[h]
                    wb = term if wb is None else wb + term
                rows = slice(i1 * PEER_NK + r, i1 * PEER_NK + r + sub)
                cd[rows, lanes] = wb * _gelu(a_t[rows, lanes]).astype(bf16)
    o_ref[...] += lax.dot_general(cd[...], v_ref[...], (((0,), (0,)), ((), ())), preferred_element_type=f32)


def _peer_dense(hf_p, hf_s, u_b, v_b, rank1, cnt, f0, e1, tm, te):
    n_main = hf_p.shape[0] // tm
    t = (n_main + 1) * tm
    ni1 = te // PEER_NK
    assert ni1 == 8, "one 8-row block of side-0 rows per expert tile"
    rows8 = pl.BlockSpec((PEER_H, ni1, tm), lambda i, j: (0, j, i))
    full = pl.BlockSpec((PEER_H, PEER_NK, tm), lambda i, j: (0, 0, i))
    return pl.pallas_call(
        _peer_dense_kernel,
        grid=(n_main + 1, PEER_E // te),
        in_specs=[pl.BlockSpec((tm, D), lambda i, j: (jnp.minimum(i, n_main - 1), 0)),
                  pl.BlockSpec((tm, D), lambda i, j: (0, 0)),
                  pl.BlockSpec((te, D), lambda i, j: (j, 0)),
                  pl.BlockSpec((te, D), lambda i, j: (j, 0)),
                  full, rows8, rows8, full],
        out_specs=pl.BlockSpec((tm, D), lambda i, j: (i, 0)),
        out_shape=jax.ShapeDtypeStruct((t, D), f32),
        scratch_shapes=[pltpu.VMEM((te, tm), bf16), pltpu.VMEM((tm, D), bf16)],
        compiler_params=_cp(("arbitrary", "arbitrary")),
        name="peer_dense",
    )(hf_p, hf_s, u_b, v_b, rank1, cnt, f0, e1)


def _final_kernel(x1_ref, p_ref, g2_ref, nw_ref, o_ref):
    x2 = x1_ref[...] + g2_ref[...] * p_ref[...]
    o_ref[...] = x2 * lax.rsqrt(jnp.mean(x2 * x2, axis=-1, keepdims=True) + EPS) * nw_ref[...]


def _final(x1, peer_out, mod, norm_w, per_token, rows_per_seq, row_off):
    t = x1.shape[0]
    tm = min(512, t)
    off = row_off // tm
    return pl.pallas_call(
        _final_kernel,
        grid=(t // tm,),
        in_specs=[pl.BlockSpec((tm, D), lambda i: (i, 0)),
                  pl.BlockSpec((tm, D), lambda i: (i + off, 0)),
                  _mod_specs(per_token, tm, rows_per_seq, 5, 1),
                  pl.BlockSpec((1, D), lambda i: (0, 0))],
        out_specs=pl.BlockSpec((tm, D), lambda i: (i, 0)),
        out_shape=jax.ShapeDtypeStruct((t, D), f32),
        compiler_params=_cp(("parallel",)),
        name="final_norm",
    )(x1, peer_out, mod, norm_w.reshape(1, D))


def _pad_row(v, lo):
    return jnp.zeros((1, 128), f32).at[0, lo:lo + v.shape[0]].set(v.astype(f32))


def _prep_weights(w_in, ssd_conv_w, ssd_conv_b, ssd_dt_bias, ssd_A_log, ssd_D, ssd_norm_w, gdn_conv_w,
                  gdn_dt_bias, gdn_A_log, gdn_norm_w):
    o_z, o_xbc = 0, SSD_INNER
    o_dt = o_xbc + SSD_CONV
    o_qkv = o_dt + SSD_H
    o_a = o_qkv + GDN_CONV
    o_b = o_a + GDN_H
    o_gate = o_b + GDN_H
    cols = [w_in[:, o_z:o_z + SSD_INNER], w_in[:, o_xbc:o_xbc + SSD_INNER], w_in[:, o_gate:o_gate + GDN_V],
            w_in[:, o_qkv:o_qkv + GDN_CONV], w_in[:, o_xbc + SSD_INNER:o_xbc + SSD_CONV],
            w_in[:, o_dt:o_dt + SSD_H], w_in[:, o_a:o_a + GDN_H], w_in[:, o_b:o_b + GDN_H]]
    w_re = jnp.concatenate(cols, axis=1)
    w_re = jnp.pad(w_re, ((0, 0), (0, P_COLS - w_re.shape[1]))).astype(bf16)
    expand = (jnp.arange(128)[:, None] == (jnp.arange(SSD_INNER)[None, :] // SSD_P)).astype(f32)
    return dict(
        w_in_re=w_re,
        ssd_wxs=ssd_conv_w[:, :SSD_INNER], ssd_wbc=ssd_conv_w[:, SSD_INNER:],
        ssd_bxs=ssd_conv_b[None, :SSD_INNER], ssd_bbc=ssd_conv_b[None, SSD_INNER:],
        ssd_cw=ssd_conv_w, ssd_cb=ssd_conv_b[None, :],
        ssd_dtb=_pad_row(ssd_dt_bias, SM_DT), ssd_alog=_pad_row(ssd_A_log, SM_DT),
        ssd_de=jnp.repeat(ssd_D.astype(f32), SSD_P)[None, :], ssd_nw=ssd_norm_w[None, :].astype(f32),
        ssd_expand=expand,
        gdn_cw=gdn_conv_w, gdn_dtb=_pad_row(gdn_dt_bias, SM_A), gdn_alog=_pad_row(gdn_A_log, SM_A),
        gdn_nw=gdn_norm_w[None, :].astype(f32),
    )


PEER_TM = 512
PEER_TE = 1024
PEER_SUB_ROWS = 16


def kernel(x_prompt, x_sample, c_prompt, c_sample, state_ssd, state_ssd_conv, state_gdn, state_gdn_conv, w_ada, b_ada, norm1_w, norm2_w, w_in, ssd_conv_w, ssd_conv_b, ssd_dt_bias, ssd_A_log, ssd_D, ssd_norm_w, gdn_conv_w, gdn_dt_bias, gdn_A_log, gdn_norm_w, w_out, peer_w_q, peer_sub_keys, peer_u, peer_v, final_norm_w):
    assert w_ada.shape[0] == 1, "single layer"
    nb, seq, _ = x_prompt.shape
    ns = x_sample.shape[0]
    assert x_sample.shape[1] == 1 and seq % SSD_CHUNK == 0 and seq % 512 == 0 and ns % 8 == 0
    w = _prep_weights(w_in[0], ssd_conv_w[0], ssd_conv_b[0], ssd_dt_bias[0], ssd_A_log[0], ssd_D[0], ssd_norm_w[0],
                      gdn_conv_w[0], gdn_dt_bias[0], gdn_A_log[0], gdn_norm_w[0])
    w_out_b = w_out[0].astype(bf16)
    wq_b = peer_w_q[0].T.astype(bf16)
    sk = peer_sub_keys[0].reshape(2 * PEER_H, PEER_NK, PEER_DK // 2).astype(bf16)
    u_b = peer_u[0].astype(bf16)
    v_b = peer_v[0].astype(bf16)

    nbp = -(-nb // 8) * 8
    c_all = jnp.concatenate([jnp.pad(c_prompt, ((0, nbp - nb), (0, 0))), c_sample], axis=0)
    mod = _modulation(c_all, w_ada[0], b_ada[0])
    mod_p = mod[:nb].reshape(nb, 1, 6 * D)
    mod_s = mod[nbp:]

    xp = x_prompt.reshape(nb * seq, D)
    xs = x_sample.reshape(ns, D)

    proj_p = _inproj(xp, mod_p, norm1_w[0], w["w_in_re"], False, seq)
    y_ssd_p, ssd_state_p = _ssd_prompt(proj_p, nb, seq, w)
    o_gdn_p, gdn_state_p = _gdn_prompt(proj_p, nb, seq, w)
    x1_p, hf_p = _outproj(y_ssd_p, o_gdn_p, xp, mod_p, norm2_w[0], w_out_b, False, seq)
    tail = proj_p.reshape(nb, seq, P_COLS)[:, seq - (CONV_W - 1):, :]
    ssd_conv_p = jnp.concatenate([tail[..., OFF_XS:OFF_XS + SSD_INNER], tail[..., OFF_BC:OFF_BC + SSD_BC]], axis=-1)
    gdn_conv_p = tail[..., OFF_QKV:OFF_QKV + GDN_CONV]

    proj_s = _inproj(xs, mod_s, norm1_w[0], w["w_in_re"], True, 1)
    y_ssd_s, ssd_conv_s, ssd_state_s = _ssd_step(proj_s, state_ssd_conv[0].reshape(ns, 3 * SSD_CONV), state_ssd[0], w)
    o_gdn_s, gdn_conv_s, gdn_state_s = _gdn_step(proj_s, state_gdn_conv[0].reshape(ns, 3 * GDN_CONV), state_gdn[0], w)
    x1_s, hf_s = _outproj(y_ssd_s, o_gdn_s, xs, mod_s, norm2_w[0], w_out_b, True, 1)

    tm = PEER_TM
    assert (nb * seq) % tm == 0 and ns <= tm
    hf_s = jnp.pad(hf_s, ((0, tm - ns), (0, 0)))
    scores_t = _peer_scores(hf_p, hf_s, wq_b, sk, tm)
    rank1, cnt, f0, e1 = _peer_select(scores_t, 128)
    peer_out = _peer_dense(hf_p, hf_s, u_b, v_b, rank1, cnt, f0, e1, tm, PEER_TE)

    y_p = _final(x1_p, peer_out, mod_p, final_norm_w, False, seq, 0)
    y_s = _final(x1_s, peer_out, mod_s, final_norm_w, True, 1, nb * seq)

    return (y_p.reshape(nb, seq, D), y_s.reshape(ns, 1, D),
            ssd_state_p[None], ssd_conv_p[None], gdn_state_p[None], gdn_conv_p[None],
            ssd_state_s[None], ssd_conv_s.reshape(1, ns, CONV_W - 1, SSD_CONV),
            gdn_state_s[None], gdn_conv_s.reshape(1, ns, CONV_W - 1, GDN_CONV))
```

```python
import functools
import math

import jax
import jax.numpy as jnp
from jax import lax
from jax.experimental import pallas as pl
from jax.experimental.pallas import tpu as pltpu

f32 = jnp.float32
bf16 = jnp.bfloat16

D = 2048
CONV_W = 4
SSD_P = 64
SSD_INNER = D // 2
SSD_H = SSD_INNER // SSD_P
SSD_G = 2
SSD_N = 128
SSD_CHUNK = 128
SSD_BC = 2 * SSD_G * SSD_N
SSD_CONV = SSD_INNER + SSD_BC
GDN_DK = 128
GDN_DV = 128
GDN_V = D // 2
GDN_H = GDN_V // GDN_DV
GDN_CHUNK = 64
GDN_CONV = GDN_H * (2 * GDN_DK + GDN_DV)
PEER_H = 8
PEER_NK = 128
PEER_E = PEER_NK * PEER_NK
PEER_DK = 256
PEER_K = 16
EPS = 1e-6
NEG = -1e30

OFF_Z, OFF_XS, OFF_GATE, OFF_QKV, OFF_BC, OFF_SM = 0, 1024, 2048, 3072, 6144, 6656
P_COLS = 7168
SM_DT, SM_A, SM_B = 0, 16, 24

VMEM_LIMIT = 56 * 1024 * 1024


def _cp(sem):
    return pltpu.CompilerParams(dimension_semantics=sem, vmem_limit_bytes=VMEM_LIMIT)


def _silu(x):
    return x * jax.nn.sigmoid(x)


def _softplus(x):
    return jnp.maximum(x, 0.0) + jnp.log1p(jnp.exp(-jnp.abs(x)))


def _dot(a, b):
    return jnp.dot(a, b, preferred_element_type=f32)


def _dot_nt(a, b):
    return lax.dot_general(a, b, (((1,), (1,)), ((), ())), preferred_element_type=f32)


def _dot_hi(a, b):
    return jnp.dot(a, b, preferred_element_type=f32, precision=lax.Precision.HIGHEST)


def _iota(shape, dim):
    return lax.broadcasted_iota(jnp.int32, shape, dim)


def _mod_kernel(c_ref, w_ref, b_ref, o_ref):
    a = _silu(c_ref[...]).astype(bf16)
    o_ref[...] = _dot(a, w_ref[...].astype(bf16)) + b_ref[...]


def _modulation(c_all, w_ada, b_ada):
    m, tn = c_all.shape[0], 1024
    return pl.pallas_call(
        _mod_kernel,
        grid=(6 * D // tn,),
        in_specs=[pl.BlockSpec((m, D), lambda j: (0, 0)),
                  pl.BlockSpec((D, tn), lambda j: (0, j)),
                  pl.BlockSpec((1, tn), lambda j: (0, j))],
        out_specs=pl.BlockSpec((m, tn), lambda j: (0, j)),
        out_shape=jax.ShapeDtypeStruct((m, 6 * D), f32),
        compiler_params=_cp(("arbitrary",)),
        name="adaln_mod",
    )(c_all, w_ada, b_ada.reshape(1, 6 * D))


def _mod_specs(per_token, tm, rows_per_seq, which, grid_rank):
    if per_token:
        if grid_rank == 2:
            return pl.BlockSpec((tm, D), lambda i, j: (i, which))
        return pl.BlockSpec((tm, D), lambda i: (i, which))
    tps = rows_per_seq // tm
    if grid_rank == 2:
        return pl.BlockSpec((None, 1, D), lambda i, j: (i // tps, 0, which))
    return pl.BlockSpec((None, 1, D), lambda i: (i // tps, 0, which))


def _inproj_kernel(x_ref, sh_ref, sc_ref, nw_ref, w_ref, o_ref, hm_ref):
    @pl.when(pl.program_id(1) == 0)
    def _():
        x = x_ref[...]
        y = x * lax.rsqrt(jnp.mean(x * x, axis=-1, keepdims=True) + EPS) * nw_ref[...]
        hm_ref[...] = (y * (1.0 + sc_ref[...]) + sh_ref[...]).astype(bf16)

    o_ref[...] = _dot(hm_ref[...], w_ref[...])


def _inproj(x, mod, norm_w, w_re, per_token, rows_per_seq):
    t = x.shape[0]
    tm = min(1024, t)
    assert per_token or rows_per_seq % tm == 0
    tn = 1024
    return pl.pallas_call(
        _inproj_kernel,
        grid=(t // tm, P_COLS // tn),
        in_specs=[pl.BlockSpec((tm, D), lambda i, j: (i, 0)),
                  _mod_specs(per_token, tm, rows_per_seq, 0, 2),
                  _mod_specs(per_token, tm, rows_per_seq, 1, 2),
                  pl.BlockSpec((1, D), lambda i, j: (0, 0)),
                  pl.BlockSpec((D, tn), lambda i, j: (0, j))],
        out_specs=pl.BlockSpec((tm, tn), lambda i, j: (i, j)),
        out_shape=jax.ShapeDtypeStruct((t, P_COLS), f32),
        scratch_shapes=[pltpu.VMEM((tm, D), bf16)],
        compiler_params=_cp(("parallel", "arbitrary")),
        name="inproj",
    )(x, mod, mod, norm_w.reshape(1, D), w_re)


def _conv_chunk(ext_ref, cur, w_ref, b_row, q):
    ext_ref[8:8 + q, :] = cur
    acc = w_ref[CONV_W - 1:CONV_W, :] * cur
    for k in range(CONV_W - 1):
        acc = acc + w_ref[k:k + 1, :] * ext_ref[5 + k:5 + k + q, :]
    ext_ref[0:8, :] = cur[q - 8:q, :]
    if b_row is not None:
        acc = acc + b_row
    return acc


def _group_rmsnorm(y, nw, width):
    outs = []
    for g in range(y.shape[-1] // width):
        seg = y[:, g * width:(g + 1) * width]
        ms = jnp.mean(seg * seg, axis=-1, keepdims=True)
        outs.append(seg * lax.rsqrt(ms + EPS) * nw[:, g * width:(g + 1) * width])
    return jnp.concatenate(outs, axis=-1)


def _lane_mask(shape, lo, hi):
    lane = _iota(shape, len(shape) - 1)
    return (lane >= lo) & (lane < hi)


def _ssd_kernel(z_ref, xs_ref, bc_ref, sm_ref, wxs_ref, wbc_ref, bxs_ref, bbc_ref, dtb_ref, alog_ref, de_ref,
                nw_ref, y_ref, st_ref, extx, extb, ht, yscr):
    q = SSD_CHUNK
    c = pl.program_id(1)

    @pl.when(c == 0)
    def _():
        extx[0:8, :] = jnp.zeros((8, SSD_INNER), f32)
        extb[0:8, :] = jnp.zeros((8, SSD_BC), f32)
        ht[...] = jnp.zeros(ht.shape, f32)

    xs = _silu(_conv_chunk(extx, xs_ref[...], wxs_ref, bxs_ref[...], q))
    bcv = _silu(_conv_chunk(extb, bc_ref[...], wbc_ref, bbc_ref[...], q))

    head_lanes = _lane_mask((q, 128), SM_DT, SM_DT + SSD_H)
    dt = jnp.where(head_lanes, _softplus(sm_ref[...] + dtb_ref[...]), 0.0)
    a_row = -jnp.exp(alog_ref[...])
    tri = (_iota((q, q), 0) >= _iota((q, q), 1))
    cs = _dot_hi(tri.astype(f32), dt * a_row)
    cs_t = cs.T
    dt_t = dt.T
    last_col = cs_t[:, q - 1:q]
    w_t = dt_t * jnp.exp(last_col - cs_t)
    ecs = jnp.exp(cs)
    lane_lo = _iota((q, 128), 1) < SSD_P

    for g in range(SSD_G):
        b_g = bcv[:, g * SSD_N:(g + 1) * SSD_N]
        c_g = bcv[:, SSD_G * SSD_N + g * SSD_N:SSD_G * SSD_N + (g + 1) * SSD_N]
        b_gt = b_g.T
        cb = _dot(c_g.astype(bf16), b_gt.astype(bf16))
        for r2 in range(SSD_H // SSD_G // 2):
            pair = g * (SSD_H // SSD_G // 2) + r2
            xs_pair = xs[:, pair * 128:(pair + 1) * 128]
            xs_pair_b = xs_pair.astype(bf16)
            h_prev = ht[pair]
            rhs = jnp.concatenate([xs_pair_b, h_prev.astype(bf16)], axis=0)
            ys, hs, els = [], [], []
            for e in range(2):
                h = 2 * pair + e
                diff = cs[:, h:h + 1] - cs_t[h:h + 1, :]
                dec = jnp.exp(jnp.where(tri, diff, NEG))
                m = cb * dec * dt_t[h:h + 1, :]
                lhs = jnp.concatenate([m, c_g * ecs[:, h:h + 1]], axis=1).astype(bf16)
                ys.append(_dot(lhs, rhs))
                bw = (b_gt * w_t[h:h + 1, :]).astype(bf16)
                hs.append(_dot(bw, xs_pair_b))
                els.append(jnp.exp(last_col[h:h + 1, :]))
            yscr[:, pair * 128:(pair + 1) * 128] = jnp.where(lane_lo, ys[0], ys[1])
            ht[pair] = h_prev * jnp.where(lane_lo, els[0], els[1]) + jnp.where(lane_lo, hs[0], hs[1])

    z = z_ref[...]
    y = (yscr[...] + de_ref[...] * xs) * _silu(z)
    y_ref[...] = _group_rmsnorm(y, nw_ref[...], SSD_INNER // SSD_G).astype(y_ref.dtype)

    @pl.when(c == pl.num_programs(1) - 1)
    def _():
        for pair in range(SSD_H // 2):
            t = ht[pair].T
            st_ref[2 * pair] = t[0:SSD_P]
            st_ref[2 * pair + 1] = t[SSD_P:2 * SSD_P]


def _ssd_prompt(proj, nb, seq, w):
    nc = seq // SSD_CHUNK
    q = SSD_CHUNK
    row = lambda n, c: n * nc + c
    const = lambda shape: pl.BlockSpec(shape, lambda n, c: (0, 0))
    return pl.pallas_call(
        _ssd_kernel,
        grid=(nb, nc),
        in_specs=[pl.BlockSpec((q, 1024), lambda n, c: (row(n, c), OFF_Z // 1024)),
                  pl.BlockSpec((q, 1024), lambda n, c: (row(n, c), OFF_XS // 1024)),
                  pl.BlockSpec((q, SSD_BC), lambda n, c: (row(n, c), OFF_BC // SSD_BC)),
                  pl.BlockSpec((q, 128), lambda n, c: (row(n, c), OFF_SM // 128)),
                  const((CONV_W, SSD_INNER)), const((CONV_W, SSD_BC)), const((1, SSD_INNER)), const((1, SSD_BC)),
                  const((1, 128)), const((1, 128)), const((1, SSD_INNER)), const((1, SSD_INNER))],
        out_specs=[pl.BlockSpec((q, SSD_INNER), lambda n, c: (row(n, c), 0)),
                   pl.BlockSpec((None, SSD_H, SSD_P, SSD_N), lambda n, c: (n, 0, 0, 0))],
        out_shape=[jax.ShapeDtypeStruct((nb * seq, SSD_INNER), bf16),
                   jax.ShapeDtypeStruct((nb, SSD_H, SSD_P, SSD_N), f32)],
        scratch_shapes=[pltpu.VMEM((q + 8, SSD_INNER), f32), pltpu.VMEM((q + 8, SSD_BC), f32),
                        pltpu.VMEM((SSD_H // 2, SSD_N, 2 * SSD_P), f32), pltpu.VMEM((q, SSD_INNER), f32)],
        compiler_params=_cp(("parallel", "arbitrary")),
        name="ssd_scan",
    )(proj, proj, proj, proj, w["ssd_wxs"], w["ssd_wbc"], w["ssd_bxs"], w["ssd_bbc"], w["ssd_dtb"],
      w["ssd_alog"], w["ssd_de"], w["ssd_nw"])


def _gdn_kernel(qkv_ref, gate_ref, sm_ref, cw_ref, dtb_ref, alog_ref, nw_ref, o_ref, st_ref, ext, s_scr):
    q = GDN_CHUNK
    c = pl.program_id(1)

    @pl.when(c == 0)
    def _():
        ext[0:8, :] = jnp.zeros((8, GDN_CONV), f32)
        s_scr[...] = jnp.zeros(s_scr.shape, f32)

    act = _silu(_conv_chunk(ext, qkv_ref[...], cw_ref, None, q))
    sm = sm_ref[...]
    a_lanes = _lane_mask((q, 128), SM_A, SM_A + GDN_H)
    g = jnp.where(a_lanes, -jnp.exp(alog_ref[...]) * _softplus(sm + dtb_ref[...]), 0.0)
    beta = jax.nn.sigmoid(sm)
    incl = _iota((q, q), 0) >= _iota((q, q), 1)
    strict = _iota((q, q), 0) > _iota((q, q), 1)
    gc = _dot_hi(incl.astype(f32), g)
    gc_t = gc.T
    egc = jnp.exp(gc)
    gate = gate_ref[...]
    nw = nw_ref[...]

    heads = range(GDN_H)
    rb, cb_ = _iota((q, q), 0), _iota((q, q), 1)
    same16 = (rb // 16) == (cb_ // 16)
    same32 = (rb // 32) == (cb_ // 32)
    diag_mask = strict & same16
    off32_mask = strict & same32 & jnp.logical_not(same16)
    off64_mask = strict & jnp.logical_not(same32)

    qn, kn, kb, dec, ecol, bcol, gcol, vh = [], [], [], [], [], [], [], []
    for h in heads:
        qh = act[:, h * GDN_DK:(h + 1) * GDN_DK]
        kh = act[:, GDN_H * GDN_DK + h * GDN_DK:GDN_H * GDN_DK + (h + 1) * GDN_DK]
        vh.append(act[:, 2 * GDN_H * GDN_DK + h * GDN_DV:2 * GDN_H * GDN_DK + (h + 1) * GDN_DV])
        qn.append(qh * lax.rsqrt(jnp.sum(qh * qh, axis=-1, keepdims=True) + EPS) * (GDN_DK ** -0.5))
        kn.append(kh * lax.rsqrt(jnp.sum(kh * kh, axis=-1, keepdims=True) + EPS))
        gcol.append(gc[:, SM_A + h:SM_A + h + 1])
        grow = gc_t[SM_A + h:SM_A + h + 1, :]
        ecol.append(egc[:, SM_A + h:SM_A + h + 1])
        bcol.append(beta[:, SM_B + h:SM_B + h + 1])
        dec.append(jnp.exp(jnp.where(incl, gcol[h] - grow, NEG)))
        kb.append(kn[h] * bcol[h])
    kn_b = [kn[h].astype(bf16) for h in heads]
    lmat = [_dot_nt(kb[h].astype(bf16), kn_b[h]) * dec[h] for h in heads]
    qk = [_dot_nt(qn[h].astype(bf16), kn_b[h]) * dec[h] for h in heads]

    dotb = lambda a, b: _dot(a.astype(bf16), b.astype(bf16))
    n_pow = [jnp.where(diag_mask, -lmat[h], 0.0) for h in heads]
    t_m = list(n_pow)
    for _ in range(3):
        n_pow = [dotb(n_pow[h], n_pow[h]) for h in heads]
        t_m = [t_m[h] + n_pow[h] + dotb(t_m[h], n_pow[h]) for h in heads]
    for mask in (off32_mask, off64_mask):
        off = [jnp.where(mask, lmat[h], 0.0) for h in heads]
        y = [off[h] + dotb(off[h], t_m[h]) for h in heads]
        t_m = [t_m[h] - (y[h] + dotb(t_m[h], y[h])) for h in heads]

    rhs = [jnp.concatenate([vh[h] * bcol[h], kb[h] * ecol[h]], axis=1) for h in heads]
    sol = [rhs[h] + dotb(t_m[h], rhs[h]) for h in heads]
    s_prev = [s_scr[h] for h in heads]
    s_b = [s_prev[h].astype(bf16) for h in heads]
    v_new = [sol[h][:, :GDN_DV] - _dot(sol[h][:, GDN_DV:].astype(bf16), s_b[h]) for h in heads]
    v_new_b = [v_new[h].astype(bf16) for h in heads]
    o = [_dot((qn[h] * ecol[h]).astype(bf16), s_b[h]) + _dot(qk[h].astype(bf16), v_new_b[h]) for h in heads]
    for h in heads:
        glast = gcol[h][q - 1:q, :]
        kdec = kn[h] * jnp.exp(glast - gcol[h])
        s_scr[h] = s_prev[h] * jnp.exp(glast) + _dot(kdec.T.astype(bf16), v_new_b[h])
    for h in heads:
        on = o[h] * lax.rsqrt(jnp.mean(o[h] * o[h], axis=-1, keepdims=True) + EPS) * nw
        o_ref[:, h * GDN_DV:(h + 1) * GDN_DV] = (on * _silu(gate[:, h * GDN_DV:(h + 1) * GDN_DV])).astype(o_ref.dtype)

    @pl.when(c == pl.num_programs(1) - 1)
    def _():
        st_ref[...] = s_scr[...]


def _gdn_prompt(proj, nb, seq, w):
    nc = seq // GDN_CHUNK
    q = GDN_CHUNK
    row = lambda n, c: n * nc + c
    const = lambda shape: pl.BlockSpec(shape, lambda n, c: (0, 0))
    return pl.pallas_call(
        _gdn_kernel,
        grid=(nb, nc),
        in_specs=[pl.BlockSpec((q, GDN_CONV), lambda n, c: (row(n, c), OFF_QKV // GDN_CONV)),
                  pl.BlockSpec((q, GDN_V), lambda n, c: (row(n, c), OFF_GATE // GDN_V)),
                  pl.BlockSpec((q, 128), lambda n, c: (row(n, c), OFF_SM // 128)),
                  const((CONV_W, GDN_CONV)), const((1, 128)), const((1, 128)), const((1, GDN_DV))],
        out_specs=[pl.BlockSpec((q, GDN_V), lambda n, c: (row(n, c), 0)),
                   pl.BlockSpec((None, GDN_H, GDN_DK, GDN_DV), lambda n, c: (n, 0, 0, 0))],
        out_shape=[jax.ShapeDtypeStruct((nb * seq, GDN_V), bf16),
                   jax.ShapeDtypeStruct((nb, GDN_H, GDN_DK, GDN_DV), f32)],
        scratch_shapes=[pltpu.VMEM((q + 8, GDN_CONV), f32), pltpu.VMEM((GDN_H, GDN_DK, GDN_DV), f32)],
        compiler_params=_cp(("parallel", "arbitrary")),
        name="gdn_scan",
    )(proj, proj, proj, w["gdn_cw"], w["gdn_dtb"], w["gdn_alog"], w["gdn_nw"])


def _conv_step(buf, xraw, w_ref, width):
    acc = w_ref[CONV_W - 1:CONV_W, :] * xraw
    for k in range(CONV_W - 1):
        acc = acc + w_ref[k:k + 1, :] * buf[:, k * width:(k + 1) * width]
    return acc, jnp.concatenate([buf[:, width:], xraw], axis=1)


def _ssd_step_kernel(z_ref, xs_ref, bc_ref, sm_ref, buf_ref, st_ref, cw_ref, cb_ref, dtb_ref, alog_ref, de_ref,
                     nw_ref, ex_ref, y_ref, nbuf_ref, nst_ref, yscr):
    bs = z_ref.shape[0]
    xraw = jnp.concatenate([xs_ref[...], bc_ref[...]], axis=1)
    conv, nbuf = _conv_step(buf_ref[...], xraw, cw_ref, SSD_CONV)
    nbuf_ref[...] = nbuf
    act = _silu(conv + cb_ref[...])
    xs = act[:, :SSD_INNER]
    head_lanes = _lane_mask((bs, 128), SM_DT, SM_DT + SSD_H)
    dt = jnp.where(head_lanes, _softplus(sm_ref[...] + dtb_ref[...]), 0.0)
    da = jnp.exp(dt * (-jnp.exp(alog_ref[...])))
    dt_e = _dot_hi(dt, ex_ref[...])
    da_e = _dot_hi(da, ex_ref[...])
    cols = jnp.concatenate([xs * dt_e, da_e], axis=0).T
    gw = SSD_INNER // SSD_G
    hg = SSD_H // SSD_G
    for b in range(bs):
        xcol = cols[:, b:b + 1]
        dcol = cols[:, bs + b:bs + b + 1]
        for g in range(SSD_G):
            b_row = act[b:b + 1, SSD_INNER + g * SSD_N:SSD_INNER + (g + 1) * SSD_N]
            c_row = act[b:b + 1, SSD_INNER + (SSD_G + g) * SSD_N:SSD_INNER + (SSD_G + g + 1) * SSD_N]
            st = st_ref[b, g * hg:(g + 1) * hg].reshape(gw, SSD_N)
            new = st * dcol[g * gw:(g + 1) * gw] + xcol[g * gw:(g + 1) * gw] * b_row
            nst_ref[b, g * hg:(g + 1) * hg] = new.reshape(hg, SSD_P, SSD_N)
            c8 = jnp.broadcast_to(c_row, (8, SSD_N))
            yscr[b:b + 1, g * gw:(g + 1) * gw] = _dot_nt(c8, new)[0:1]
    y = (yscr[...] + de_ref[...] * xs) * _silu(z_ref[...])
    y_ref[...] = _group_rmsnorm(y, nw_ref[...], gw).astype(y_ref.dtype)


def _ssd_step(proj, buf, state, w, bs=8):
    n = proj.shape[0]
    const = lambda shape: pl.BlockSpec(shape, lambda i: (0,) * len(shape))
    return pl.pallas_call(
        _ssd_step_kernel,
        grid=(n // bs,),
        in_specs=[pl.BlockSpec((bs, 1024), lambda i: (i, OFF_Z // 1024)),
                  pl.BlockSpec((bs, 1024), lambda i: (i, OFF_XS // 1024)),
                  pl.BlockSpec((bs, SSD_BC), lambda i: (i, OFF_BC // SSD_BC)),
                  pl.BlockSpec((bs, 128), lambda i: (i, OFF_SM // 128)),
                  pl.BlockSpec((bs, 3 * SSD_CONV), lambda i: (i, 0)),
                  pl.BlockSpec((bs, SSD_H, SSD_P, SSD_N), lambda i: (i, 0, 0, 0)),
                  const((CONV_W, SSD_CONV)), const((1, SSD_CONV)), const((1, 128)), const((1, 128)),
                  const((1, SSD_INNER)), const((1, SSD_INNER)), const((128, SSD_INNER))],
        out_specs=[pl.BlockSpec((bs, SSD_INNER), lambda i: (i, 0)),
                   pl.BlockSpec((bs, 3 * SSD_CONV), lambda i: (i, 0)),
                   pl.BlockSpec((bs, SSD_H, SSD_P, SSD_N), lambda i: (i, 0, 0, 0))],
        out_shape=[jax.ShapeDtypeStruct((n, SSD_INNER), bf16),
                   jax.ShapeDtypeStruct((n, 3 * SSD_CONV), f32),
                   jax.ShapeDtypeStruct((n, SSD_H, SSD_P, SSD_N), f32)],
        scratch_shapes=[pltpu.VMEM((bs, SSD_INNER), f32)],
        compiler_params=_cp(("parallel",)),
        name="ssd_step",
    )(proj, proj, proj, proj, buf, state, w["ssd_cw"], w["ssd_cb"], w["ssd_dtb"], w["ssd_alog"], w["ssd_de"],
      w["ssd_nw"], w["ssd_expand"])


def _gdn_step_kernel(qkv_ref, gate_ref, sm_ref, buf_ref, st_ref, cw_ref, dtb_ref, alog_ref, nw_ref,
                     o_ref, nbuf_ref, nst_ref, oscr):
    bs = qkv_ref.shape[0]
    conv, nbuf = _conv_step(buf_ref[...], qkv_ref[...], cw_ref, GDN_CONV)
    nbuf_ref[...] = nbuf
    act = _silu(conv)
    sm = sm_ref[...]
    eg = jnp.exp(-jnp.exp(alog_ref[...]) * _softplus(sm + dtb_ref[...]))
    beta = jax.nn.sigmoid(sm)
    qs, ks = [], []
    for h in range(GDN_H):
        qh = act[:, h * GDN_DK:(h + 1) * GDN_DK]
        kh = act[:, GDN_H * GDN_DK + h * GDN_DK:GDN_H * GDN_DK + (h + 1) * GDN_DK]
        qs.append(qh * lax.rsqrt(jnp.sum(qh * qh, axis=-1, keepdims=True) + EPS) * (GDN_DK ** -0.5))
        ks.append(kh * lax.rsqrt(jnp.sum(kh * kh, axis=-1, keepdims=True) + EPS))
    qk_t = jnp.concatenate(qs + ks, axis=1).T
    for b in range(bs):
        for h in range(GDN_H):
            qcol = qk_t[h * GDN_DK:(h + 1) * GDN_DK, b:b + 1]
            kcol = qk_t[(GDN_H + h) * GDN_DK:(GDN_H + h + 1) * GDN_DK, b:b + 1]
            vrow = act[b:b + 1, 2 * GDN_H * GDN_DK + h * GDN_DV:2 * GDN_H * GDN_DK + (h + 1) * GDN_DV]
            egs = eg[b:b + 1, SM_A + h:SM_A + h + 1]
            bet = beta[b:b + 1, SM_B + h:SM_B + h + 1]
            s = st_ref[b, h]
            v_new = bet * (vrow - egs * jnp.sum(kcol * s, axis=0, keepdims=True))
            s_new = s * egs + kcol * v_new
            nst_ref[b, h] = s_new
            oscr[b:b + 1, h * GDN_DV:(h + 1) * GDN_DV] = jnp.sum(qcol * s_new, axis=0, keepdims=True)
    o = oscr[...]
    nw = nw_ref[...]
    gate = gate_ref[...]
    for h in range(GDN_H):
        oh = o[:, h * GDN_DV:(h + 1) * GDN_DV]
        on = oh * lax.rsqrt(jnp.mean(oh * oh, axis=-1, keepdims=True) + EPS) * nw
        o_ref[:, h * GDN_DV:(h + 1) * GDN_DV] = (on * _silu(gate[:, h * GDN_DV:(h + 1) * GDN_DV])).astype(o_ref.dtype)


def _gdn_step(proj, buf, state, w, bs=8):
    n = proj.shape[0]
    const = lambda shape: pl.BlockSpec(shape, lambda i: (0,) * len(shape))
    return pl.pallas_call(
        _gdn_step_kernel,
        grid=(n // bs,),
        in_specs=[pl.BlockSpec((bs, GDN_CONV), lambda i: (i, OFF_QKV // GDN_CONV)),
                  pl.BlockSpec((bs, GDN_V), lambda i: (i, OFF_GATE // GDN_V)),
                  pl.BlockSpec((bs, 128), lambda i: (i, OFF_SM // 128)),
                  pl.BlockSpec((bs, 3 * GDN_CONV), lambda i: (i, 0)),
                  pl.BlockSpec((bs, GDN_H, GDN_DK, GDN_DV), lambda i: (i, 0, 0, 0)),
                  const((CONV_W, GDN_CONV)), const((1, 128)), const((1, 128)), const((1, GDN_DV))],
        out_specs=[pl.BlockSpec((bs, GDN_V), lambda i: (i, 0)),
                   pl.BlockSpec((bs, 3 * GDN_CONV), lambda i: (i, 0)),
                   pl.BlockSpec((bs, GDN_H, GDN_DK, GDN_DV), lambda i: (i, 0, 0, 0))],
        out_shape=[jax.ShapeDtypeStruct((n, GDN_V), bf16),
                   jax.ShapeDtypeStruct((n, 3 * GDN_CONV), f32),
                   jax.ShapeDtypeStruct((n, GDN_H, GDN_DK, GDN_DV), f32)],
        scratch_shapes=[pltpu.VMEM((bs, GDN_V), f32)],
        compiler_params=_cp(("parallel",)),
        name="gdn_step",
    )(proj, proj, proj, buf, state, w["gdn_cw"], w["gdn_dtb"], w["gdn_alog"], w["gdn_nw"])


def _outproj_kernel(ys_ref, og_ref, x_ref, g1_ref, sh_ref, sc_ref, nw_ref, w_ref, x1_ref, hf_ref):
    m = _dot(ys_ref[...], w_ref[0:SSD_INNER, :]) + _dot(og_ref[...], w_ref[SSD_INNER:, :])
    x1 = x_ref[...] + g1_ref[...] * m
    x1_ref[...] = x1
    y = x1 * lax.rsqrt(jnp.mean(x1 * x1, axis=-1, keepdims=True) + EPS) * nw_ref[...]
    hf_ref[...] = (y * (1.0 + sc_ref[...]) + sh_ref[...]).astype(hf_ref.dtype)


def _outproj(ys, og, x, mod, norm_w, w_out_b, per_token, rows_per_seq):
    t = x.shape[0]
    tm = min(512, t)
    return pl.pallas_call(
        _outproj_kernel,
        grid=(t // tm,),
        in_specs=[pl.BlockSpec((tm, SSD_INNER), lambda i: (i, 0)),
                  pl.BlockSpec((tm, GDN_V), lambda i: (i, 0)),
                  pl.BlockSpec((tm, D), lambda i: (i, 0)),
                  _mod_specs(per_token, tm, rows_per_seq, 2, 1),
                  _mod_specs(per_token, tm, rows_per_seq, 3, 1),
                  _mod_specs(per_token, tm, rows_per_seq, 4, 1),
                  pl.BlockSpec((1, D), lambda i: (0, 0)),
                  pl.BlockSpec((D, D), lambda i: (0, 0))],
        out_specs=[pl.BlockSpec((tm, D), lambda i: (i, 0)), pl.BlockSpec((tm, D), lambda i: (i, 0))],
        out_shape=[jax.ShapeDtypeStruct((t, D), f32), jax.ShapeDtypeStruct((t, D), bf16)],
        compiler_params=_cp(("parallel",)),
        name="outproj",
    )(ys, og, x, mod, mod, mod, norm_w.reshape(1, D), w_out_b)


def _pick_token_tile(i, n_main, main_ref, tail_ref, dst_ref):
    @pl.when(i < n_main)
    def _():
        dst_ref[...] = main_ref[...]

    @pl.when(i >= n_main)
    def _():
        dst_ref[...] = tail_ref[...]


def _peer_scores_kernel(hfp_ref, hfs_ref, wq_ref, sk_ref, o_ref, hf_scr):
    _pick_token_tile(pl.program_id(0), pl.num_programs(0) - 1, hfp_ref, hfs_ref, hf_scr)
    q_t = _dot_nt(wq_ref[...], hf_scr[...])
    half = PEER_DK // 2
    for hs in range(2 * PEER_H):
        o_ref[hs] = _dot(sk_ref[hs], q_t[hs * half:(hs + 1) * half].astype(bf16))


def _peer_scores(hf_p, hf_s, wq_b, sk, tm):
    n_main = hf_p.shape[0] // tm
    t = (n_main + 1) * tm
    return pl.pallas_call(
        _peer_scores_kernel,
        grid=(n_main + 1,),
        in_specs=[pl.BlockSpec((tm, D), lambda i: (jnp.minimum(i, n_main - 1), 0)),
                  pl.BlockSpec((tm, D), lambda i: (0, 0)),
                  pl.BlockSpec((PEER_H * PEER_DK, D), lambda i: (0, 0)),
                  pl.BlockSpec((2 * PEER_H, PEER_NK, PEER_DK // 2), lambda i: (0, 0, 0))],
        out_specs=pl.BlockSpec((2 * PEER_H, PEER_NK, tm), lambda i: (0, 0, i)),
        out_shape=jax.ShapeDtypeStruct((2 * PEER_H, PEER_NK, t), f32),
        scratch_shapes=[pltpu.VMEM((tm, D), bf16)],
        compiler_params=_cp(("arbitrary",)),
        name="peer_scores",
    )(hf_p, hf_s, wq_b, sk)


_STAIR = [PEER_K // (k1 + 1) for k1 in range(PEER_K)]
_STAIR_ROWS = -(-sum(_STAIR) // 8) * 8


def _peer_select_kernel(s_ref, rank_ref, cnt_ref, f0_ref, e1_ref, v1_scr, cand):
    tm = s_ref.shape[-1]
    for h in range(PEER_H):
        s0 = s_ref[2 * h]
        s1 = s_ref[2 * h + 1]
        tops = []
        rank1 = jnp.full((PEER_NK, tm), float(PEER_K), f32)
        for side, cur in enumerate((s0, s1)):
            vals = []
            for k in range(PEER_K):
                m = jnp.max(cur, axis=0, keepdims=True)
                vals.append(m)
                hit = cur == m
                if side == 1:
                    rank1 = jnp.where(hit, float(k), rank1)
                cur = jnp.where(hit, -jnp.inf, cur)
            tops.append(vals)
        for k in range(PEER_K):
            v1_scr[k:k + 1, :] = tops[1][k]
        cand[_STAIR_ROWS - 8:_STAIR_ROWS, :] = jnp.full((8, tm), -jnp.inf, f32)
        off = 0
        for k1 in range(PEER_K):
            cand[off:off + _STAIR[k1], :] = tops[0][k1] + v1_scr[0:_STAIR[k1], :]
            off += _STAIR[k1]
        cv = cand[...]
        cur = cv
        for _ in range(PEER_K):
            tau = jnp.max(cur, axis=0, keepdims=True)
            cur = jnp.where(cur == tau, -jnp.inf, cur)
        m0, m1 = tops[0][0], tops[1][0]
        zsum = jnp.sum(jnp.where(cv >= tau, jnp.exp(cv - (m0 + m1)), 0.0), axis=0, keepdims=True)
        cnt = jnp.zeros((PEER_NK, tm), f32)
        for k1 in range(PEER_K):
            v1 = v1_scr[0:_STAIR[k1], :]
            n_k = jnp.sum(jnp.where(tops[0][k1] + v1 >= tau, 1.0, 0.0), axis=0, keepdims=True)
            cnt = jnp.where(s0 == tops[0][k1], n_k, cnt)
        rank_ref[h] = rank1.astype(bf16)
        cnt_ref[h] = cnt
        f0_ref[h] = jnp.exp(s0 - m0) / zsum
        e1_ref[h] = jnp.exp(s1 - m1).astype(bf16)


def _peer_select(scores_t, tm):
    t = scores_t.shape[-1]
    spec = pl.BlockSpec((PEER_H, PEER_NK, tm), lambda i: (0, 0, i))
    shape = lambda dt: jax.ShapeDtypeStruct((PEER_H, PEER_NK, t), dt)
    return pl.pallas_call(
        _peer_select_kernel,
        grid=(t // tm,),
        in_specs=[pl.BlockSpec((2 * PEER_H, PEER_NK, tm), lambda i: (0, 0, i))],
        out_specs=[spec, spec, spec, spec],
        out_shape=[shape(bf16), shape(f32), shape(f32), shape(bf16)],
        scratch_shapes=[pltpu.VMEM((PEER_K, tm), f32), pltpu.VMEM((_STAIR_ROWS, tm), f32)],
        compiler_params=_cp(("parallel",)),
        name="peer_select",
    )(scores_t)


def _gelu(x):
    return 0.5 * x * (1.0 + lax.erf(x * (1.0 / math.sqrt(2.0))))


def _peer_dense_kernel(hfp_ref, hfs_ref, u_ref, v_ref, rank_ref, cnt_ref, f0_ref, e1_ref, o_ref, cd, hf_scr):
    tm = hf_scr.shape[0]
    te = u_ref.shape[0]
    sub = PEER_SUB_ROWS

    @pl.when(pl.program_id(1) == 0)
    def _():
        o_ref[...] = jnp.zeros(o_ref.shape, f32)
        _pick_token_tile(pl.program_id(0), pl.num_programs(0) - 1, hfp_ref, hfs_ref, hf_scr)

    a_t = _dot_nt(u_ref[...], hf_scr[...])
    for i1 in range(te // PEER_NK):
        for c in range(0, tm, 128):
            lanes = slice(c, c + 128)
            cnt_b = [jnp.broadcast_to(cnt_ref[h, i1:i1 + 1, lanes], (sub, 128)).astype(bf16) for h in range(PEER_H)]
            f0_b = [jnp.broadcast_to(f0_ref[h, i1:i1 + 1, lanes], (sub, 128)).astype(bf16) for h in range(PEER_H)]
            for r in range(0, PEER_NK, sub):
                wb = None
                for h in range(PEER_H):
                    keep = rank_ref[h, r:r + sub, lanes] < cnt_b[h]
                    term = jnp.where(keep, e1_ref[h, r:r + sub, lanes], jnp.zeros((), bf16)) * f0_b[h]
                    wb = term if wb is None else wb + term
                rows = slice(i1 * PEER_NK + r, i1 * PEER_NK + r + sub)
                cd[rows, lanes] = wb * _gelu(a_t[rows, lanes]).astype(bf16)
    o_ref[...] += lax.dot_general(cd[...], v_ref[...], (((0,), (0,)), ((), ())), preferred_element_type=f32)


def _peer_dense(hf_p, hf_s, u_b, v_b, rank1, cnt, f0, e1, tm, te):
    n_main = hf_p.shape[0] // tm
    t = (n_main + 1) * tm
    ni1 = te // PEER_NK
    assert ni1 == 8, "one 8-row block of side-0 rows per expert tile"
    rows8 = pl.BlockSpec((PEER_H, ni1, tm), lambda i, j: (0, j, i))
    full = pl.BlockSpec((PEER_H, PEER_NK, tm), lambda i, j: (0, 0, i))
    return pl.pallas_call(
        _peer_dense_kernel,
        grid=(n_main + 1, PEER_E // te),
        in_specs=[pl.BlockSpec((tm, D), lambda i, j: (jnp.minimum(i, n_main - 1), 0)),
                  pl.BlockSpec((tm, D), lambda i, j: (0, 0)),
                  pl.BlockSpec((te, D), lambda i, j: (j, 0)),
                  pl.BlockSpec((te, D), lambda i, j: (j, 0)),
                  full, rows8, rows8, full],
        out_specs=pl.BlockSpec((tm, D), lambda i, j: (i, 0)),
        out_shape=jax.ShapeDtypeStruct((t, D), f32),
        scratch_shapes=[pltpu.VMEM((te, tm), bf16), pltpu.VMEM((tm, D), bf16)],
        compiler_params=_cp(("arbitrary", "arbitrary")),
        name="peer_dense",
    )(hf_p, hf_s, u_b, v_b, rank1, cnt, f0, e1)


def _final_kernel(x1_ref, p_ref, g2_ref, nw_ref, o_ref):
    x2 = x1_ref[...] + g2_ref[...] * p_ref[...]
    o_ref[...] = x2 * lax.rsqrt(jnp.mean(x2 * x2, axis=-1, keepdims=True) + EPS) * nw_ref[...]


def _final(x1, peer_out, mod, norm_w, per_token, rows_per_seq, row_off):
    t = x1.shape[0]
    tm = min(512, t)
    off = row_off // tm
    return pl.pallas_call(
        _final_kernel,
        grid=(t // tm,),
        in_specs=[pl.BlockSpec((tm, D), lambda i: (i, 0)),
                  pl.BlockSpec((tm, D), lambda i: (i + off, 0)),
                  _mod_specs(per_token, tm, rows_per_seq, 5, 1),
                  pl.BlockSpec((1, D), lambda i: (0, 0))],
        out_specs=pl.BlockSpec((tm, D), lambda i: (i, 0)),
        out_shape=jax.ShapeDtypeStruct((t, D), f32),
        compiler_params=_cp(("parallel",)),
        name="final_norm",
    )(x1, peer_out, mod, norm_w.reshape(1, D))


def _pad_row(v, lo):
    return jnp.zeros((1, 128), f32).at[0, lo:lo + v.shape[0]].set(v.astype(f32))


def _prep_weights(w_in, ssd_conv_w, ssd_conv_b, ssd_dt_bias, ssd_A_log, ssd_D, ssd_norm_w, gdn_conv_w,
                  gdn_dt_bias, gdn_A_log, gdn_norm_w):
    o_z, o_xbc = 0, SSD_INNER
    o_dt = o_xbc + SSD_CONV
    o_qkv = o_dt + SSD_H
    o_a = o_qkv + GDN_CONV
    o_b = o_a + GDN_H
    o_gate = o_b + GDN_H
    cols = [w_in[:, o_z:o_z + SSD_INNER], w_in[:, o_xbc:o_xbc + SSD_INNER], w_in[:, o_gate:o_gate + GDN_V],
            w_in[:, o_qkv:o_qkv + GDN_CONV], w_in[:, o_xbc + SSD_INNER:o_xbc + SSD_CONV],
            w_in[:, o_dt:o_dt + SSD_H], w_in[:, o_a:o_a + GDN_H], w_in[:, o_b:o_b + GDN_H]]
    used = sum(c.shape[1] for c in cols)
    w_re = jnp.concatenate([c.astype(bf16) for c in cols] + [jnp.zeros((D, P_COLS - used), bf16)], axis=1)
    expand = (jnp.arange(128)[:, None] == (jnp.arange(SSD_INNER)[None, :] // SSD_P)).astype(f32)
    return dict(
        w_in_re=w_re,
        ssd_wxs=ssd_conv_w[:, :SSD_INNER], ssd_wbc=ssd_conv_w[:, SSD_INNER:],
        ssd_bxs=ssd_conv_b[None, :SSD_INNER], ssd_bbc=ssd_conv_b[None, SSD_INNER:],
        ssd_cw=ssd_conv_w, ssd_cb=ssd_conv_b[None, :],
        ssd_dtb=_pad_row(ssd_dt_bias, SM_DT), ssd_alog=_pad_row(ssd_A_log, SM_DT),
        ssd_de=jnp.repeat(ssd_D.astype(f32), SSD_P)[None, :], ssd_nw=ssd_norm_w[None, :].astype(f32),
        ssd_expand=expand,
        gdn_cw=gdn_conv_w, gdn_dtb=_pad_row(gdn_dt_bias, SM_A), gdn_alog=_pad_row(gdn_A_log, SM_A),
        gdn_nw=gdn_norm_w[None, :].astype(f32),
    )


PEER_TM = 512
PEER_TE = 1024
PEER_SUB_ROWS = 16


def kernel(x_prompt, x_sample, c_prompt, c_sample, state_ssd, state_ssd_conv, state_gdn, state_gdn_conv, w_ada, b_ada, norm1_w, norm2_w, w_in, ssd_conv_w, ssd_conv_b, ssd_dt_bias, ssd_A_log, ssd_D, ssd_norm_w, gdn_conv_w, gdn_dt_bias, gdn_A_log, gdn_norm_w, w_out, peer_w_q, peer_sub_keys, peer_u, peer_v, final_norm_w):
    assert w_ada.shape[0] == 1, "single layer"
    nb, seq, _ = x_prompt.shape
    ns = x_sample.shape[0]
    assert x_sample.shape[1] == 1 and seq % SSD_CHUNK == 0 and seq % 512 == 0 and ns % 8 == 0
    w = _prep_weights(w_in[0], ssd_conv_w[0], ssd_conv_b[0], ssd_dt_bias[0], ssd_A_log[0], ssd_D[0], ssd_norm_w[0],
                      gdn_conv_w[0], gdn_dt_bias[0], gdn_A_log[0], gdn_norm_w[0])
    w_out_b = w_out[0].astype(bf16)
    wq_b = peer_w_q[0].T.astype(bf16)
    sk = peer_sub_keys[0].reshape(2 * PEER_H, PEER_NK, PEER_DK // 2).astype(bf16)
    u_b = peer_u[0].astype(bf16)
    v_b = peer_v[0].astype(bf16)

    nbp = -(-nb // 8) * 8
    c_all = jnp.concatenate([jnp.pad(c_prompt, ((0, nbp - nb), (0, 0))), c_sample], axis=0)
    mod = _modulation(c_all, w_ada[0], b_ada[0])
    mod_p = mod[:nb].reshape(nb, 1, 6 * D)
    mod_s = mod[nbp:]

    xp = x_prompt.reshape(nb * seq, D)
    xs = x_sample.reshape(ns, D)

    proj_p = _inproj(xp, mod_p, norm1_w[0], w["w_in_re"], False, seq)
    y_ssd_p, ssd_state_p = _ssd_prompt(proj_p, nb, seq, w)
    o_gdn_p, gdn_state_p = _gdn_prompt(proj_p, nb, seq, w)
    x1_p, hf_p = _outproj(y_ssd_p, o_gdn_p, xp, mod_p, norm2_w[0], w_out_b, False, seq)
    tail = proj_p.reshape(nb, seq, P_COLS)[:, seq - (CONV_W - 1):, :]
    ssd_conv_p = jnp.concatenate([tail[..., OFF_XS:OFF_XS + SSD_INNER], tail[..., OFF_BC:OFF_BC + SSD_BC]], axis=-1)
    gdn_conv_p = tail[..., OFF_QKV:OFF_QKV + GDN_CONV]

    proj_s = _inproj(xs, mod_s, norm1_w[0], w["w_in_re"], True, 1)
    y_ssd_s, ssd_conv_s, ssd_state_s = _ssd_step(proj_s, state_ssd_conv[0].reshape(ns, 3 * SSD_CONV), state_ssd[0], w)
    o_gdn_s, gdn_conv_s, gdn_state_s = _gdn_step(proj_s, state_gdn_conv[0].reshape(ns, 3 * GDN_CONV), state_gdn[0], w)
    x1_s, hf_s = _outproj(y_ssd_s, o_gdn_s, xs, mod_s, norm2_w[0], w_out_b, True, 1)

    tm = PEER_TM
    assert (nb * seq) % tm == 0 and ns <= tm
    hf_s = jnp.pad(hf_s, ((0, tm - ns), (0, 0)))
    scores_t = _peer_scores(hf_p, hf_s, wq_b, sk, tm)
    rank1, cnt, f0, e1 = _peer_select(scores_t, 256)
    peer_out = _peer_dense(hf_p, hf_s, u_b, v_b, rank1, cnt, f0, e1, tm, PEER_TE)

    y_p = _final(x1_p, peer_out, mod_p, final_norm_w, False, seq, 0)
    y_s = _final(x1_s, peer_out, mod_s, final_norm_w, True, 1, nb * seq)

    return (y_p.reshape(nb, seq, D), y_s.reshape(ns, 1, D),
            ssd_state_p[None], ssd_conv_p[None], gdn_state_p[None], gdn_conv_p[None],
            ssd_state_s[None], ssd_conv_s.reshape(1, ns, CONV_W - 1, SSD_CONV),
            gdn_state_s[None], gdn_conv_s.reshape(1, ns, CONV_W - 1, GDN_CONV))
```

```python
import functools
import math

import jax
import jax.numpy as jnp
from jax import lax
from jax.experimental import pallas as pl
from jax.experimental.pallas import tpu as pltpu

f32 = jnp.float32
bf16 = jnp.bfloat16

D = 2048
CONV_W = 4
SSD_P = 64
SSD_INNER = D // 2
SSD_H = SSD_INNER // SSD_P
SSD_G = 2
SSD_N = 128
SSD_CHUNK = 128
SSD_BC = 2 * SSD_G * SSD_N
SSD_CONV = SSD_INNER + SSD_BC
GDN_DK = 128
GDN_DV = 128
GDN_V = D // 2
GDN_H = GDN_V // GDN_DV
GDN_CHUNK = 64
GDN_CONV = GDN_H * (2 * GDN_DK + GDN_DV)
PEER_H = 8
PEER_NK = 128
PEER_E = PEER_NK * PEER_NK
PEER_DK = 256
PEER_K = 16
EPS = 1e-6
NEG = -1e30

OFF_Z, OFF_XS, OFF_GATE, OFF_QKV, OFF_BC, OFF_SM = 0, 1024, 2048, 3072, 6144, 6656
P_COLS = 7168
SM_DT, SM_A, SM_B = 0, 16, 24

VMEM_LIMIT = 56 * 1024 * 1024


def _cp(sem):
    return pltpu.CompilerParams(dimension_semantics=sem, vmem_limit_bytes=VMEM_LIMIT)


def _silu(x):
    return x * jax.nn.sigmoid(x)


def _softplus(x):
    return jnp.maximum(x, 0.0) + jnp.log1p(jnp.exp(-jnp.abs(x)))


def _dot(a, b):
    return jnp.dot(a, b, preferred_element_type=f32)


def _dot_nt(a, b):
    return lax.dot_general(a, b, (((1,), (1,)), ((), ())), preferred_element_type=f32)


def _dot_hi(a, b):
    return jnp.dot(a, b, preferred_element_type=f32, precision=lax.Precision.HIGHEST)


def _iota(shape, dim):
    return lax.broadcasted_iota(jnp.int32, shape, dim)


def _mod_kernel(c_ref, w_ref, b_ref, o_ref):
    a = _silu(c_ref[...]).astype(bf16)
    o_ref[...] = _dot(a, w_ref[...].astype(bf16)) + b_ref[...]


def _modulation(c_all, w_ada, b_ada):
    m, tn = c_all.shape[0], 1024
    return pl.pallas_call(
        _mod_kernel,
        grid=(6 * D // tn,),
        in_specs=[pl.BlockSpec((m, D), lambda j: (0, 0)),
                  pl.BlockSpec((D, tn), lambda j: (0, j)),
                  pl.BlockSpec((1, tn), lambda j: (0, j))],
        out_specs=pl.BlockSpec((m, tn), lambda j: (0, j)),
        out_shape=jax.ShapeDtypeStruct((m, 6 * D), f32),
        compiler_params=_cp(("arbitrary",)),
        name="adaln_mod",
    )(c_all, w_ada, b_ada.reshape(1, 6 * D))


def _mod_specs(per_token, tm, rows_per_seq, which, grid_rank):
    if per_token:
        if grid_rank == 2:
            return pl.BlockSpec((tm, D), lambda i, j: (i, which))
        return pl.BlockSpec((tm, D), lambda i: (i, which))
    tps = rows_per_seq // tm
    if grid_rank == 2:
        return pl.BlockSpec((None, 1, D), lambda i, j: (i // tps, 0, which))
    return pl.BlockSpec((None, 1, D), lambda i: (i // tps, 0, which))


def _inproj_kernel(x_ref, sh_ref, sc_ref, nw_ref, w_ref, o_ref, hm_ref):
    @pl.when(pl.program_id(1) == 0)
    def _():
        x = x_ref[...]
        y = x * lax.rsqrt(jnp.mean(x * x, axis=-1, keepdims=True) + EPS) * nw_ref[...]
        hm_ref[...] = (y * (1.0 + sc_ref[...]) + sh_ref[...]).astype(bf16)

    o_ref[...] = _dot(hm_ref[...], w_ref[...])


def _inproj(x, mod, norm_w, w_re, per_token, rows_per_seq):
    t = x.shape[0]
    tm = min(1024, t if per_token else rows_per_seq)
    assert t % tm == 0 and (per_token or rows_per_seq % tm == 0)
    tn = 1024
    return pl.pallas_call(
        _inproj_kernel,
        grid=(t // tm, P_COLS // tn),
        in_specs=[pl.BlockSpec((tm, D), lambda i, j: (i, 0)),
                  _mod_specs(per_token, tm, rows_per_seq, 0, 2),
                  _mod_specs(per_token, tm, rows_per_seq, 1, 2),
                  pl.BlockSpec((1, D), lambda i, j: (0, 0)),
                  pl.BlockSpec((D, tn), lambda i, j: (0, j))],
        out_specs=pl.BlockSpec((tm, tn), lambda i, j: (i, j)),
        out_shape=jax.ShapeDtypeStruct((t, P_COLS), f32),
        scratch_shapes=[pltpu.VMEM((tm, D), bf16)],
        compiler_params=_cp(("parallel", "arbitrary")),
        name="inproj",
    )(x, mod, mod, norm_w.reshape(1, D), w_re)


def _conv_chunk(ext_ref, cur, w_ref, b_row, q):
    ext_ref[8:8 + q, :] = cur
    acc = w_ref[CONV_W - 1:CONV_W, :] * cur
    for k in range(CONV_W - 1):
        acc = acc + w_ref[k:k + 1, :] * ext_ref[5 + k:5 + k + q, :]
    ext_ref[0:8, :] = cur[q - 8:q, :]
    if b_row is not None:
        acc = acc + b_row
    return acc


def _group_rmsnorm(y, nw, width):
    outs = []
    for g in range(y.shape[-1] // width):
        seg = y[:, g * width:(g + 1) * width]
        ms = jnp.mean(seg * seg, axis=-1, keepdims=True)
        outs.append(seg * lax.rsqrt(ms + EPS) * nw[:, g * width:(g + 1) * width])
    return jnp.concatenate(outs, axis=-1)


def _lane_mask(shape, lo, hi):
    lane = _iota(shape, len(shape) - 1)
    return (lane >= lo) & (lane < hi)


def _ssd_kernel(z_ref, xs_ref, bc_ref, sm_ref, wxs_ref, wbc_ref, bxs_ref, bbc_ref, dtb_ref, alog_ref, de_ref,
                nw_ref, y_ref, st_ref, extx, extb, ht, yscr):
    q = SSD_CHUNK
    c = pl.program_id(1)

    @pl.when(c == 0)
    def _():
        extx[0:8, :] = jnp.zeros((8, SSD_INNER), f32)
        extb[0:8, :] = jnp.zeros((8, SSD_BC), f32)
        ht[...] = jnp.zeros(ht.shape, f32)

    xs = _silu(_conv_chunk(extx, xs_ref[...], wxs_ref, bxs_ref[...], q))
    bcv = _silu(_conv_chunk(extb, bc_ref[...], wbc_ref, bbc_ref[...], q))

    head_lanes = _lane_mask((q, 128), SM_DT, SM_DT + SSD_H)
    dt = jnp.where(head_lanes, _softplus(sm_ref[...] + dtb_ref[...]), 0.0)
    a_row = -jnp.exp(alog_ref[...])
    tri = (_iota((q, q), 0) >= _iota((q, q), 1))
    cs = _dot_hi(tri.astype(f32), dt * a_row)
    cs_t = cs.T
    dt_t = dt.T
    last_col = cs_t[:, q - 1:q]
    w_t = dt_t * jnp.exp(last_col - cs_t)
    ecs = jnp.exp(cs)
    lane_lo = _iota((q, 128), 1) < SSD_P

    for g in range(SSD_G):
        b_g = bcv[:, g * SSD_N:(g + 1) * SSD_N]
        c_g = bcv[:, SSD_G * SSD_N + g * SSD_N:SSD_G * SSD_N + (g + 1) * SSD_N]
        b_gt = b_g.T
        cb = _dot(c_g.astype(bf16), b_gt.astype(bf16))
        for r2 in range(SSD_H // SSD_G // 2):
            pair = g * (SSD_H // SSD_G // 2) + r2
            xs_pair = xs[:, pair * 128:(pair + 1) * 128]
            xs_pair_b = xs_pair.astype(bf16)
            h_prev = ht[pair]
            rhs = jnp.concatenate([xs_pair_b, h_prev.astype(bf16)], axis=0)
            ys, hs, els = [], [], []
            for e in range(2):
                h = 2 * pair + e
                diff = cs[:, h:h + 1] - cs_t[h:h + 1, :]
                dec = jnp.exp(jnp.where(tri, diff, NEG))
                m = cb * dec * dt_t[h:h + 1, :]
                lhs = jnp.concatenate([m, c_g * ecs[:, h:h + 1]], axis=1).astype(bf16)
                ys.append(_dot(lhs, rhs))
                bw = (b_gt * w_t[h:h + 1, :]).astype(bf16)
                hs.append(_dot(bw, xs_pair_b))
                els.append(jnp.exp(last_col[h:h + 1, :]))
            yscr[:, pair * 128:(pair + 1) * 128] = jnp.where(lane_lo, ys[0], ys[1])
            ht[pair] = h_prev * jnp.where(lane_lo, els[0], els[1]) + jnp.where(lane_lo, hs[0], hs[1])

    z = z_ref[...]
    y = (yscr[...] + de_ref[...] * xs) * _silu(z)
    y_ref[...] = _group_rmsnorm(y, nw_ref[...], SSD_INNER // SSD_G).astype(y_ref.dtype)

    @pl.when(c == pl.num_programs(1) - 1)
    def _():
        for pair in range(SSD_H // 2):
            t = ht[pair].T
            st_ref[2 * pair] = t[0:SSD_P]
            st_ref[2 * pair + 1] = t[SSD_P:2 * SSD_P]


def _ssd_prompt(proj, nb, seq, w):
    nc = seq // SSD_CHUNK
    q = SSD_CHUNK
    row = lambda n, c: n * nc + c
    const = lambda shape: pl.BlockSpec(shape, lambda n, c: (0, 0))
    return pl.pallas_call(
        _ssd_kernel,
        grid=(nb, nc),
        in_specs=[pl.BlockSpec((q, 1024), lambda n, c: (row(n, c), OFF_Z // 1024)),
                  pl.BlockSpec((q, 1024), lambda n, c: (row(n, c), OFF_XS // 1024)),
                  pl.BlockSpec((q, SSD_BC), lambda n, c: (row(n, c), OFF_BC // SSD_BC)),
                  pl.BlockSpec((q, 128), lambda n, c: (row(n, c), OFF_SM // 128)),
                  const((CONV_W, SSD_INNER)), const((CONV_W, SSD_BC)), const((1, SSD_INNER)), const((1, SSD_BC)),
                  const((1, 128)), const((1, 128)), const((1, SSD_INNER)), const((1, SSD_INNER))],
        out_specs=[pl.BlockSpec((q, SSD_INNER), lambda n, c: (row(n, c), 0)),
                   pl.BlockSpec((None, SSD_H, SSD_P, SSD_N), lambda n, c: (n, 0, 0, 0))],
        out_shape=[jax.ShapeDtypeStruct((nb * seq, SSD_INNER), bf16),
                   jax.ShapeDtypeStruct((nb, SSD_H, SSD_P, SSD_N), f32)],
        scratch_shapes=[pltpu.VMEM((q + 8, SSD_INNER), f32), pltpu.VMEM((q + 8, SSD_BC), f32),
                        pltpu.VMEM((SSD_H // 2, SSD_N, 2 * SSD_P), f32), pltpu.VMEM((q, SSD_INNER), f32)],
        compiler_params=_cp(("parallel", "arbitrary")),
        name="ssd_scan",
    )(proj, proj, proj, proj, w["ssd_wxs"], w["ssd_wbc"], w["ssd_bxs"], w["ssd_bbc"], w["ssd_dtb"],
      w["ssd_alog"], w["ssd_de"], w["ssd_nw"])


def _gdn_kernel(qkv_ref, gate_ref, sm_ref, cw_ref, dtb_ref, alog_ref, nw_ref, o_ref, st_ref, ext, s_scr):
    q = GDN_CHUNK
    n_seq = qkv_ref.shape[0]
    c = pl.program_id(1)

    @pl.when(c == 0)
    def _():
        for s in range(n_seq):
            ext[s, 0:8, :] = jnp.zeros((8, GDN_CONV), f32)
        s_scr[...] = jnp.zeros(s_scr.shape, f32)

    incl = _iota((q, q), 0) >= _iota((q, q), 1)
    strict = _iota((q, q), 0) > _iota((q, q), 1)
    rb, cb_ = _iota((q, q), 0), _iota((q, q), 1)
    same16 = (rb // 16) == (cb_ // 16)
    same32 = (rb // 32) == (cb_ // 32)
    diag_mask = strict & same16
    off32_mask = strict & same32 & jnp.logical_not(same16)
    off64_mask = strict & jnp.logical_not(same32)
    a_lanes = _lane_mask((q, 128), SM_A, SM_A + GDN_H)
    nw = nw_ref[...]

    heads = range(n_seq * GDN_H)
    qn, kn, kb, dec, ecol, bcol, gcol, vh = [], [], [], [], [], [], [], []
    for s in range(n_seq):
        act = _silu(_conv_chunk(ext.at[s], qkv_ref[s], cw_ref, None, q))
        sm = sm_ref[s]
        g = jnp.where(a_lanes, -jnp.exp(alog_ref[...]) * _softplus(sm + dtb_ref[...]), 0.0)
        beta = jax.nn.sigmoid(sm)
        gc = _dot_hi(incl.astype(f32), g)
        gc_t = gc.T
        egc = jnp.exp(gc)
        for h in range(GDN_H):
            qh = act[:, h * GDN_DK:(h + 1) * GDN_DK]
            kh = act[:, GDN_H * GDN_DK + h * GDN_DK:GDN_H * GDN_DK + (h + 1) * GDN_DK]
            vh.append(act[:, 2 * GDN_H * GDN_DK + h * GDN_DV:2 * GDN_H * GDN_DK + (h + 1) * GDN_DV])
            qn.append(qh * lax.rsqrt(jnp.sum(qh * qh, axis=-1, keepdims=True) + EPS) * (GDN_DK ** -0.5))
            kn.append(kh * lax.rsqrt(jnp.sum(kh * kh, axis=-1, keepdims=True) + EPS))
            gcol.append(gc[:, SM_A + h:SM_A + h + 1])
            grow = gc_t[SM_A + h:SM_A + h + 1, :]
            ecol.append(egc[:, SM_A + h:SM_A + h + 1])
            bcol.append(beta[:, SM_B + h:SM_B + h + 1])
            dec.append(jnp.exp(jnp.where(incl, gcol[-1] - grow, NEG)))
            kb.append(kn[-1] * bcol[-1])
    kn_b = [kn[h].astype(bf16) for h in heads]
    lmat = [_dot_nt(kb[h].astype(bf16), kn_b[h]) * dec[h] for h in heads]
    qk = [_dot_nt(qn[h].astype(bf16), kn_b[h]) * dec[h] for h in heads]

    dotb = lambda a, b: _dot(a.astype(bf16), b.astype(bf16))
    n_pow = [jnp.where(diag_mask, -lmat[h], 0.0) for h in heads]
    t_m = list(n_pow)
    for _ in range(3):
        n_pow = [dotb(n_pow[h], n_pow[h]) for h in heads]
        t_m = [t_m[h] + n_pow[h] + dotb(t_m[h], n_pow[h]) for h in heads]
    for mask in (off32_mask, off64_mask):
        off = [jnp.where(mask, lmat[h], 0.0) for h in heads]
        y = [off[h] + dotb(off[h], t_m[h]) for h in heads]
        t_m = [t_m[h] - (y[h] + dotb(t_m[h], y[h])) for h in heads]

    rhs = [jnp.concatenate([vh[h] * bcol[h], kb[h] * ecol[h]], axis=1) for h in heads]
    sol = [rhs[h] + dotb(t_m[h], rhs[h]) for h in heads]
    s_prev = [s_scr[h] for h in heads]
    s_b = [s_prev[h].astype(bf16) for h in heads]
    v_new = [sol[h][:, :GDN_DV] - _dot(sol[h][:, GDN_DV:].astype(bf16), s_b[h]) for h in heads]
    v_new_b = [v_new[h].astype(bf16) for h in heads]
    o = [_dot((qn[h] * ecol[h]).astype(bf16), s_b[h]) + _dot(qk[h].astype(bf16), v_new_b[h]) for h in heads]
    for h in heads:
        glast = gcol[h][q - 1:q, :]
        kdec = kn[h] * jnp.exp(glast - gcol[h])
        s_scr[h] = s_prev[h] * jnp.exp(glast) + _dot(kdec.T.astype(bf16), v_new_b[h])
    for s in range(n_seq):
        gate = gate_ref[s]
        for h in range(GDN_H):
            oh = o[s * GDN_H + h]
            on = oh * lax.rsqrt(jnp.mean(oh * oh, axis=-1, keepdims=True) + EPS) * nw
            o_ref[s, :, h * GDN_DV:(h + 1) * GDN_DV] = (
                on * _silu(gate[:, h * GDN_DV:(h + 1) * GDN_DV])).astype(o_ref.dtype)

    @pl.when(c == pl.num_programs(1) - 1)
    def _():
        for s in range(n_seq):
            st_ref[s] = s_scr[s * GDN_H:(s + 1) * GDN_H]


def _gdn_prompt(proj, nb, seq, w):
    nc = seq // GDN_CHUNK
    q = GDN_CHUNK
    n_seq = 2 if nb % 2 == 0 else 1
    proj3 = proj.reshape(nb, seq, P_COLS)
    const = lambda shape: pl.BlockSpec(shape, lambda n, c: (0, 0))
    o, st = pl.pallas_call(
        _gdn_kernel,
        grid=(nb // n_seq, nc),
        in_specs=[pl.BlockSpec((n_seq, q, GDN_CONV), lambda n, c: (n, c, OFF_QKV // GDN_CONV)),
                  pl.BlockSpec((n_seq, q, GDN_V), lambda n, c: (n, c, OFF_GATE // GDN_V)),
                  pl.BlockSpec((n_seq, q, 128), lambda n, c: (n, c, OFF_SM // 128)),
                  const((CONV_W, GDN_CONV)), const((1, 128)), const((1, 128)), const((1, GDN_DV))],
        out_specs=[pl.BlockSpec((n_seq, q, GDN_V), lambda n, c: (n, c, 0)),
                   pl.BlockSpec((n_seq, GDN_H, GDN_DK, GDN_DV), lambda n, c: (n, 0, 0, 0))],
        out_shape=[jax.ShapeDtypeStruct((nb, seq, GDN_V), bf16),
                   jax.ShapeDtypeStruct((nb, GDN_H, GDN_DK, GDN_DV), f32)],
        scratch_shapes=[pltpu.VMEM((n_seq, q + 8, GDN_CONV), f32),
                        pltpu.VMEM((n_seq * GDN_H, GDN_DK, GDN_DV), f32)],
        compiler_params=_cp(("parallel", "arbitrary")),
        name="gdn_scan",
    )(proj3, proj3, proj3, w["gdn_cw"], w["gdn_dtb"], w["gdn_alog"], w["gdn_nw"])
    return o.reshape(nb * seq, GDN_V), st


def _conv_step(buf, xraw, w_ref, width):
    acc = w_ref[CONV_W - 1:CONV_W, :] * xraw
    for k in range(CONV_W - 1):
        acc = acc + w_ref[k:k + 1, :] * buf[:, k * width:(k + 1) * width]
    return acc, jnp.concatenate([buf[:, width:], xraw], axis=1)


def _ssd_step_kernel(z_ref, xs_ref, bc_ref, sm_ref, buf_ref, st_ref, cw_ref, cb_ref, dtb_ref, alog_ref, de_ref,
                     nw_ref, ex_ref, y_ref, nbuf_ref, nst_ref, yscr):
    bs = z_ref.shape[0]
    xraw = jnp.concatenate([xs_ref[...], bc_ref[...]], axis=1)
    conv, nbuf = _conv_step(buf_ref[...], xraw, cw_ref, SSD_CONV)
    nbuf_ref[...] = nbuf
    act = _silu(conv + cb_ref[...])
    xs = act[:, :SSD_INNER]
    head_lanes = _lane_mask((bs, 128), SM_DT, SM_DT + SSD_H)
    dt = jnp.where(head_lanes, _softplus(sm_ref[...] + dtb_ref[...]), 0.0)
    da = jnp.exp(dt * (-jnp.exp(alog_ref[...])))
    dt_e = _dot_hi(dt, ex_ref[...])
    da_e = _dot_hi(da, ex_ref[...])
    cols = jnp.concatenate([xs * dt_e, da_e], axis=0).T
    gw = SSD_INNER // SSD_G
    hg = SSD_H // SSD_G
    for b in range(bs):
        xcol = cols[:, b:b + 1]
        dcol = cols[:, bs + b:bs + b + 1]
        for g in range(SSD_G):
            b_row = act[b:b + 1, SSD_INNER + g * SSD_N:SSD_INNER + (g + 1) * SSD_N]
            c_row = act[b:b + 1, SSD_INNER + (SSD_G + g) * SSD_N:SSD_INNER + (SSD_G + g + 1) * SSD_N]
            st = st_ref[b, g * hg:(g + 1) * hg].reshape(gw, SSD_N)
            new = st * dcol[g * gw:(g + 1) * gw] + xcol[g * gw:(g + 1) * gw] * b_row
            nst_ref[b, g * hg:(g + 1) * hg] = new.reshape(hg, SSD_P, SSD_N)
            c8 = jnp.broadcast_to(c_row, (8, SSD_N))
            yscr[b:b + 1, g * gw:(g + 1) * gw] = _dot_nt(c8, new)[0:1]
    y = (yscr[...] + de_ref[...] * xs) * _silu(z_ref[...])
    y_ref[...] = _group_rmsnorm(y, nw_ref[...], gw).astype(y_ref.dtype)


def _ssd_step(proj, buf, state, w, bs=8):
    n = proj.shape[0]
    const = lambda shape: pl.BlockSpec(shape, lambda i: (0,) * len(shape))
    return pl.pallas_call(
        _ssd_step_kernel,
        grid=(n // bs,),
        in_specs=[pl.BlockSpec((bs, 1024), lambda i: (i, OFF_Z // 1024)),
                  pl.BlockSpec((bs, 1024), lambda i: (i, OFF_XS // 1024)),
                  pl.BlockSpec((bs, SSD_BC), lambda i: (i, OFF_BC // SSD_BC)),
                  pl.BlockSpec((bs, 128), lambda i: (i, OFF_SM // 128)),
                  pl.BlockSpec((bs, 3 * SSD_CONV), lambda i: (i, 0)),
                  pl.BlockSpec((bs, SSD_H, SSD_P, SSD_N), lambda i: (i, 0, 0, 0)),
                  const((CONV_W, SSD_CONV)), const((1, SSD_CONV)), const((1, 128)), const((1, 128)),
                  const((1, SSD_INNER)), const((1, SSD_INNER)), const((128, SSD_INNER))],
        out_specs=[pl.BlockSpec((bs, SSD_INNER), lambda i: (i, 0)),
                   pl.BlockSpec((bs, 3 * SSD_CONV), lambda i: (i, 0)),
                   pl.BlockSpec((bs, SSD_H, SSD_P, SSD_N), lambda i: (i, 0, 0, 0))],
        out_shape=[jax.ShapeDtypeStruct((n, SSD_INNER), bf16),
                   jax.ShapeDtypeStruct((n, 3 * SSD_CONV), f32),
                   jax.ShapeDtypeStruct((n, SSD_H, SSD_P, SSD_N), f32)],
        scratch_shapes=[pltpu.VMEM((bs, SSD_INNER), f32)],
        compiler_params=_cp(("parallel",)),
        name="ssd_step",
    )(proj, proj, proj, proj, buf, state, w["ssd_cw"], w["ssd_cb"], w["ssd_dtb"], w["ssd_alog"], w["ssd_de"],
      w["ssd_nw"], w["ssd_expand"])


def _gdn_step_kernel(qkv_ref, gate_ref, sm_ref, buf_ref, st_ref, cw_ref, dtb_ref, alog_ref, nw_ref,
                     o_ref, nbuf_ref, nst_ref, oscr):
    bs = qkv_ref.shape[0]
    conv, nbuf = _conv_step(buf_ref[...], qkv_ref[...], cw_ref, GDN_CONV)
    nbuf_ref[...] = nbuf
    act = _silu(conv)
    sm = sm_ref[...]
    eg = jnp.exp(-jnp.exp(alog_ref[...]) * _softplus(sm + dtb_ref[...]))
    beta = jax.nn.sigmoid(sm)
    qs, ks = [], []
    for h in range(GDN_H):
        qh = act[:, h * GDN_DK:(h + 1) * GDN_DK]
        kh = act[:, GDN_H * GDN_DK + h * GDN_DK:GDN_H * GDN_DK + (h + 1) * GDN_DK]
        qs.append(qh * lax.rsqrt(jnp.sum(qh * qh, axis=-1, keepdims=True) + EPS) * (GDN_DK ** -0.5))
        ks.append(kh * lax.rsqrt(jnp.sum(kh * kh, axis=-1, keepdims=True) + EPS))
    qk_t = jnp.concatenate(qs + ks, axis=1).T
    for b in range(bs):
        for h in range(GDN_H):
            qcol = qk_t[h * GDN_DK:(h + 1) * GDN_DK, b:b + 1]
            kcol = qk_t[(GDN_H + h) * GDN_DK:(GDN_H + h + 1) * GDN_DK, b:b + 1]
            vrow = act[b:b + 1, 2 * GDN_H * GDN_DK + h * GDN_DV:2 * GDN_H * GDN_DK + (h + 1) * GDN_DV]
            egs = eg[b:b + 1, SM_A + h:SM_A + h + 1]
            bet = beta[b:b + 1, SM_B + h:SM_B + h + 1]
            s = st_ref[b, h]
            v_new = bet * (vrow - egs * jnp.sum(kcol * s, axis=0, keepdims=True))
            s_new = s * egs + kcol * v_new
            nst_ref[b, h] = s_new
            oscr[b:b + 1, h * GDN_DV:(h + 1) * GDN_DV] = jnp.sum(qcol * s_new, axis=0, keepdims=True)
    o = oscr[...]
    nw = nw_ref[...]
    gate = gate_ref[...]
    for h in range(GDN_H):
        oh = o[:, h * GDN_DV:(h + 1) * GDN_DV]
        on = oh * lax.rsqrt(jnp.mean(oh * oh, axis=-1, keepdims=True) + EPS) * nw
        o_ref[:, h * GDN_DV:(h + 1) * GDN_DV] = (on * _silu(gate[:, h * GDN_DV:(h + 1) * GDN_DV])).astype(o_ref.dtype)


def _gdn_step(proj, buf, state, w, bs=8):
    n = proj.shape[0]
    const = lambda shape: pl.BlockSpec(shape, lambda i: (0,) * len(shape))
    return pl.pallas_call(
        _gdn_step_kernel,
        grid=(n // bs,),
        in_specs=[pl.BlockSpec((bs, GDN_CONV), lambda i: (i, OFF_QKV // GDN_CONV)),
                  pl.BlockSpec((bs, GDN_V), lambda i: (i, OFF_GATE // GDN_V)),
                  pl.BlockSpec((bs, 128), lambda i: (i, OFF_SM // 128)),
                  pl.BlockSpec((bs, 3 * GDN_CONV), lambda i: (i, 0)),
                  pl.BlockSpec((bs, GDN_H, GDN_DK, GDN_DV), lambda i: (i, 0, 0, 0)),
                  const((CONV_W, GDN_CONV)), const((1, 128)), const((1, 128)), const((1, GDN_DV))],
        out_specs=[pl.BlockSpec((bs, GDN_V), lambda i: (i, 0)),
                   pl.BlockSpec((bs, 3 * GDN_CONV), lambda i: (i, 0)),
                   pl.BlockSpec((bs, GDN_H, GDN_DK, GDN_DV), lambda i: (i, 0, 0, 0))],
        out_shape=[jax.ShapeDtypeStruct((n, GDN_V), bf16),
                   jax.ShapeDtypeStruct((n, 3 * GDN_CONV), f32),
                   jax.ShapeDtypeStruct((n, GDN_H, GDN_DK, GDN_DV), f32)],
        scratch_shapes=[pltpu.VMEM((bs, GDN_V), f32)],
        compiler_params=_cp(("parallel",)),
        name="gdn_step",
    )(proj, proj, proj, buf, state, w["gdn_cw"], w["gdn_dtb"], w["gdn_alog"], w["gdn_nw"])


def _outproj_kernel(ys_ref, og_ref, x_ref, g1_ref, sh_ref, sc_ref, nw_ref, w_ref, x1_ref, hf_ref):
    m = _dot(ys_ref[...], w_ref[0:SSD_INNER, :]) + _dot(og_ref[...], w_ref[SSD_INNER:, :])
    x1 = x_ref[...] + g1_ref[...] * m
    x1_ref[...] = x1
    y = x1 * lax.rsqrt(jnp.mean(x1 * x1, axis=-1, keepdims=True) + EPS) * nw_ref[...]
    hf_ref[...] = (y * (1.0 + sc_ref[...]) + sh_ref[...]).astype(hf_ref.dtype)


def _outproj(ys, og, x, mod, norm_w, w_out_b, per_token, rows_per_seq):
    t = x.shape[0]
    tm = min(512, t)
    return pl.pallas_call(
        _outproj_kernel,
        grid=(t // tm,),
        in_specs=[pl.BlockSpec((tm, SSD_INNER), lambda i: (i, 0)),
                  pl.BlockSpec((tm, GDN_V), lambda i: (i, 0)),
                  pl.BlockSpec((tm, D), lambda i: (i, 0)),
                  _mod_specs(per_token, tm, rows_per_seq, 2, 1),
                  _mod_specs(per_token, tm, rows_per_seq, 3, 1),
                  _mod_specs(per_token, tm, rows_per_seq, 4, 1),
                  pl.BlockSpec((1, D), lambda i: (0, 0)),
                  pl.BlockSpec((D, D), lambda i: (0, 0))],
        out_specs=[pl.BlockSpec((tm, D), lambda i: (i, 0)), pl.BlockSpec((tm, D), lambda i: (i, 0))],
        out_shape=[jax.ShapeDtypeStruct((t, D), f32), jax.ShapeDtypeStruct((t, D), bf16)],
        compiler_params=_cp(("parallel",)),
        name="outproj",
    )(ys, og, x, mod, mod, mod, norm_w.reshape(1, D), w_out_b)


def _pick_token_tile(i, n_main, main_ref, tail_ref, dst_ref):
    @pl.when(i < n_main)
    def _():
        dst_ref[...] = main_ref[...]

    @pl.when(i >= n_main)
    def _():
        dst_ref[...] = tail_ref[...]


def _peer_scores_kernel(hfp_ref, hfs_ref, wq_ref, sk_ref, o_ref, hf_scr):
    _pick_token_tile(pl.program_id(0), pl.num_programs(0) - 1, hfp_ref, hfs_ref, hf_scr)
    q_t = _dot_nt(wq_ref[...], hf_scr[...])
    half = PEER_DK // 2
    for hs in range(2 * PEER_H):
        o_ref[hs] = _dot(sk_ref[hs], q_t[hs * half:(hs + 1) * half].astype(bf16))


def _peer_scores(hf_p, hf_s, wq_b, sk, tm):
    n_main = hf_p.shape[0] // tm
    t = (n_main + 1) * tm
    return pl.pallas_call(
        _peer_scores_kernel,
        grid=(n_main + 1,),
        in_specs=[pl.BlockSpec((tm, D), lambda i: (jnp.minimum(i, n_main - 1), 0)),
                  pl.BlockSpec((tm, D), lambda i: (0, 0)),
                  pl.BlockSpec((PEER_H * PEER_DK, D), lambda i: (0, 0)),
                  pl.BlockSpec((2 * PEER_H, PEER_NK, PEER_DK // 2), lambda i: (0, 0, 0))],
        out_specs=pl.BlockSpec((2 * PEER_H, PEER_NK, tm), lambda i: (0, 0, i)),
        out_shape=jax.ShapeDtypeStruct((2 * PEER_H, PEER_NK, t), f32),
        scratch_shapes=[pltpu.VMEM((tm, D), bf16)],
        compiler_params=_cp(("arbitrary",)),
        name="peer_scores",
    )(hf_p, hf_s, wq_b, sk)


_STAIR = [PEER_K // (k1 + 1) for k1 in range(PEER_K)]
_STAIR_ROWS = -(-sum(_STAIR) // 8) * 8


def _peer_select_kernel(s_ref, rank_ref, cnt_ref, f0_ref, e1_ref, v1_scr, cand):
    tm = s_ref.shape[-1]
    for h in range(PEER_H):
        s0 = s_ref[2 * h]
        s1 = s_ref[2 * h + 1]
        tops = []
        rank1 = jnp.full((PEER_NK, tm), float(PEER_K), f32)
        for side, cur in enumerate((s0, s1)):
            vals = []
            for k in range(PEER_K):
                m = jnp.max(cur, axis=0, keepdims=True)
                vals.append(m)
                hit = cur == m
                if side == 1:
                    rank1 = jnp.where(hit, float(k), rank1)
                cur = jnp.where(hit, -jnp.inf, cur)
            tops.append(vals)
        for k in range(PEER_K):
            v1_scr[k:k + 1, :] = tops[1][k]
        cand[_STAIR_ROWS - 8:_STAIR_ROWS, :] = jnp.full((8, tm), -jnp.inf, f32)
        off = 0
        for k1 in range(PEER_K):
            cand[off:off + _STAIR[k1], :] = tops[0][k1] + v1_scr[0:_STAIR[k1], :]
            off += _STAIR[k1]
        cv = cand[...]
        cur = cv
        for _ in range(PEER_K):
            tau = jnp.max(cur, axis=0, keepdims=True)
            cur = jnp.where(cur == tau, -jnp.inf, cur)
        m0, m1 = tops[0][0], tops[1][0]
        zsum = jnp.sum(jnp.where(cv >= tau, jnp.exp(cv - (m0 + m1)), 0.0), axis=0, keepdims=True)
        cnt = jnp.zeros((PEER_NK, tm), f32)
        for k1 in range(PEER_K):
            v1 = v1_scr[0:_STAIR[k1], :]
            n_k = jnp.sum(jnp.where(tops[0][k1] + v1 >= tau, 1.0, 0.0), axis=0, keepdims=True)
            cnt = jnp.where(s0 == tops[0][k1], n_k, cnt)
        rank_ref[h] = rank1.astype(bf16)
        cnt_ref[h] = cnt
        f0_ref[h] = jnp.exp(s0 - m0) / zsum
        e1_ref[h] = jnp.exp(s1 - m1).astype(bf16)


def _peer_select(scores_t, tm):
    t = scores_t.shape[-1]
    spec = pl.BlockSpec((PEER_H, PEER_NK, tm), lambda i: (0, 0, i))
    shape = lambda dt: jax.ShapeDtypeStruct((PEER_H, PEER_NK, t), dt)
    return pl.pallas_call(
        _peer_select_kernel,
        grid=(t // tm,),
        in_specs=[pl.BlockSpec((2 * PEER_H, PEER_NK, tm), lambda i: (0, 0, i))],
        out_specs=[spec, spec, spec, spec],
        out_shape=[shape(bf16), shape(f32), shape(f32), shape(bf16)],
        scratch_shapes=[pltpu.VMEM((PEER_K, tm), f32), pltpu.VMEM((_STAIR_ROWS, tm), f32)],
        compiler_params=_cp(("parallel",)),
        name="peer_select",
    )(scores_t)


def _gelu(x):
    return 0.5 * x * (1.0 + lax.erf(x * (1.0 / math.sqrt(2.0))))


def _peer_dense_kernel(hfp_ref, hfs_ref, u_ref, v_ref, rank_ref, cnt_ref, f0_ref, e1_ref, o_ref, cd, hf_scr):
    tm = hf_scr.shape[0]
    te = u_ref.shape[0]
    sub = PEER_SUB_ROWS

    @pl.when(pl.program_id(1) == 0)
    def _():
        o_ref[...] = jnp.zeros(o_ref.shape, f32)
        _pick_token_tile(pl.program_id(0), pl.num_programs(0) - 1, hfp_ref, hfs_ref, hf_scr)

    a_t = _dot_nt(u_ref[...], hf_scr[...])
    for i1 in range(te // PEER_NK):
        for c in range(0, tm, 128):
            lanes = slice(c, c + 128)
            cnt_b = [jnp.broadcast_to(cnt_ref[h, i1:i1 + 1, lanes], (sub, 128)).astype(bf16) for h in range(PEER_H)]
            f0_b = [jnp.broadcast_to(f0_ref[h, i1:i1 + 1, lanes], (sub, 128)).astype(bf16) for h in range(PEER_H)]
            for r in range(0, PEER_NK, sub):
                wb = None
                for h in range(PEER_H):
                    keep = rank_ref[h, r:r + sub, lanes] < cnt_b[h]
                    term = jnp.where(keep, e1_ref[h, r:r + sub, lanes], jnp.zeros((), bf16)) * f0_b[h]
                    wb = term if wb is None else wb + term
                rows = slice(i1 * PEER_NK + r, i1 * PEER_NK + r + sub)
                cd[rows, lanes] = wb * _gelu(a_t[rows, lanes]).astype(bf16)
    o_ref[...] += lax.dot_general(cd[...], v_ref[...], (((0,), (0,)), ((), ())), preferred_element_type=f32)


def _peer_dense(hf_p, hf_s, u_b, v_b, rank1, cnt, f0, e1, tm, te):
    n_main = hf_p.shape[0] // tm
    t = (n_main + 1) * tm
    ni1 = te // PEER_NK
    assert ni1 == 8, "one 8-row block of side-0 rows per expert tile"
    rows8 = pl.BlockSpec((PEER_H, ni1, tm), lambda i, j: (0, j, i))
    full = pl.BlockSpec((PEER_H, PEER_NK, tm), lambda i, j: (0, 0, i))
    return pl.pallas_call(
        _peer_dense_kernel,
        grid=(n_main + 1, PEER_E // te),
        in_specs=[pl.BlockSpec((tm, D), lambda i, j: (jnp.minimum(i, n_main - 1), 0)),
                  pl.BlockSpec((tm, D), lambda i, j: (0, 0)),
                  pl.BlockSpec((te, D), lambda i, j: (j, 0)),
                  pl.BlockSpec((te, D), lambda i, j: (j, 0)),
                  full, rows8, rows8, full],
        out_specs=pl.BlockSpec((tm, D), lambda i, j: (i, 0)),
        out_shape=jax.ShapeDtypeStruct((t, D), f32),
        scratch_shapes=[pltpu.VMEM((te, tm), bf16), pltpu.VMEM((tm, D), bf16)],
        compiler_params=_cp(("arbitrary", "arbitrary")),
        name="peer_dense",
    )(hf_p, hf_s, u_b, v_b, rank1, cnt, f0, e1)


def _final_kernel(x1_ref, p_ref, g2_ref, nw_ref, o_ref):
    x2 = x1_ref[...] + g2_ref[...] * p_ref[...]
    o_ref[...] = x2 * lax.rsqrt(jnp.mean(x2 * x2, axis=-1, keepdims=True) + EPS) * nw_ref[...]


def _final(x1, peer_out, mod, norm_w, per_token, rows_per_seq, row_off):
    t = x1.shape[0]
    tm = min(512, t)
    off = row_off // tm
    return pl.pallas_call(
        _final_kernel,
        grid=(t // tm,),
        in_specs=[pl.BlockSpec((tm, D), lambda i: (i, 0)),
                  pl.BlockSpec((tm, D), lambda i: (i + off, 0)),
                  _mod_specs(per_token, tm, rows_per_seq, 5, 1),
                  pl.BlockSpec((1, D), lambda i: (0, 0))],
        out_specs=pl.BlockSpec((tm, D), lambda i: (i, 0)),
        out_shape=jax.ShapeDtypeStruct((t, D), f32),
        compiler_params=_cp(("parallel",)),
        name="final_norm",
    )(x1, peer_out, mod, norm_w.reshape(1, D))


def _pad_row(v, lo):
    return jnp.zeros((1, 128), f32).at[0, lo:lo + v.shape[0]].set(v.astype(f32))


def _prep_weights(w_in, ssd_conv_w, ssd_conv_b, ssd_dt_bias, ssd_A_log, ssd_D, ssd_norm_w, gdn_conv_w,
                  gdn_dt_bias, gdn_A_log, gdn_norm_w):
    o_z, o_xbc = 0, SSD_INNER
    o_dt = o_xbc + SSD_CONV
    o_qkv = o_dt + SSD_H
    o_a = o_qkv + GDN_CONV
    o_b = o_a + GDN_H
    o_gate = o_b + GDN_H
    cols = [w_in[:, o_z:o_z + SSD_INNER], w_in[:, o_xbc:o_xbc + SSD_INNER], w_in[:, o_gate:o_gate + GDN_V],
            w_in[:, o_qkv:o_qkv + GDN_CONV], w_in[:, o_xbc + SSD_INNER:o_xbc + SSD_CONV],
            w_in[:, o_dt:o_dt + SSD_H], w_in[:, o_a:o_a + GDN_H], w_in[:, o_b:o_b + GDN_H]]
    used = sum(c.shape[1] for c in cols)
    w_re = jnp.concatenate([c.astype(bf16) for c in cols] + [jnp.zeros((D, P_COLS - used), bf16)], axis=1)
    expand = (jnp.arange(128)[:, None] == (jnp.arange(SSD_INNER)[None, :] // SSD_P)).astype(f32)
    return dict(
        w_in_re=w_re,
        ssd_wxs=ssd_conv_w[:, :SSD_INNER], ssd_wbc=ssd_conv_w[:, SSD_INNER:],
        ssd_bxs=ssd_conv_b[None, :SSD_INNER], ssd_bbc=ssd_conv_b[None, SSD_INNER:],
        ssd_cw=ssd_conv_w, ssd_cb=ssd_conv_b[None, :],
        ssd_dtb=_pad_row(ssd_dt_bias, SM_DT), ssd_alog=_pad_row(ssd_A_log, SM_DT),
        ssd_de=jnp.repeat(ssd_D.astype(f32), SSD_P)[None, :], ssd_nw=ssd_norm_w[None, :].astype(f32),
        ssd_expand=expand,
        gdn_cw=gdn_conv_w, gdn_dtb=_pad_row(gdn_dt_bias, SM_A), gdn_alog=_pad_row(gdn_A_log, SM_A),
        gdn_nw=gdn_norm_w[None, :].astype(f32),
    )


PEER_TM = 512
PEER_TE = 1024
PEER_SUB_ROWS = 16


def kernel(x_prompt, x_sample, c_prompt, c_sample, state_ssd, state_ssd_conv, state_gdn, state_gdn_conv, w_ada, b_ada, norm1_w, norm2_w, w_in, ssd_conv_w, ssd_conv_b, ssd_dt_bias, ssd_A_log, ssd_D, ssd_norm_w, gdn_conv_w, gdn_dt_bias, gdn_A_log, gdn_norm_w, w_out, peer_w_q, peer_sub_keys, peer_u, peer_v, final_norm_w):
    assert w_ada.shape[0] == 1, "single layer"
    nb, seq, _ = x_prompt.shape
    ns = x_sample.shape[0]
    assert x_sample.shape[1] == 1 and seq % SSD_CHUNK == 0 and seq % 512 == 0 and ns % 8 == 0
    w = _prep_weights(w_in[0], ssd_conv_w[0], ssd_conv_b[0], ssd_dt_bias[0], ssd_A_log[0], ssd_D[0], ssd_norm_w[0],
                      gdn_conv_w[0], gdn_dt_bias[0], gdn_A_log[0], gdn_norm_w[0])
    w_out_b = w_out[0].astype(bf16)
    wq_b = peer_w_q[0].T.astype(bf16)
    sk = peer_sub_keys[0].reshape(2 * PEER_H, PEER_NK, PEER_DK // 2).astype(bf16)
    u_b = peer_u[0].astype(bf16)
    v_b = peer_v[0].astype(bf16)

    nbp = -(-nb // 8) * 8
    c_all = jnp.concatenate([jnp.pad(c_prompt, ((0, nbp - nb), (0, 0))), c_sample], axis=0)
    mod = _modulation(c_all, w_ada[0], b_ada[0])
    mod_p = mod[:nb].reshape(nb, 1, 6 * D)
    mod_s = mod[nbp:]

    xp = x_prompt.reshape(nb * seq, D)
    xs = x_sample.reshape(ns, D)

    proj_p = _inproj(xp, mod_p, norm1_w[0], w["w_in_re"], False, seq)
    y_ssd_p, ssd_state_p = _ssd_prompt(proj_p, nb, seq, w)
    o_gdn_p, gdn_state_p = _gdn_prompt(proj_p, nb, seq, w)
    x1_p, hf_p = _outproj(y_ssd_p, o_gdn_p, xp, mod_p, norm2_w[0], w_out_b, False, seq)
    tail = proj_p.reshape(nb, seq, P_COLS)[:, seq - (CONV_W - 1):, :]
    ssd_conv_p = jnp.concatenate([tail[..., OFF_XS:OFF_XS + SSD_INNER], tail[..., OFF_BC:OFF_BC + SSD_BC]], axis=-1)
    gdn_conv_p = tail[..., OFF_QKV:OFF_QKV + GDN_CONV]

    proj_s = _inproj(xs, mod_s, norm1_w[0], w["w_in_re"], True, 1)
    y_ssd_s, ssd_conv_s, ssd_state_s = _ssd_step(proj_s, state_ssd_conv[0].reshape(ns, 3 * SSD_CONV), state_ssd[0], w)
    o_gdn_s, gdn_conv_s, gdn_state_s = _gdn_step(proj_s, state_gdn_conv[0].reshape(ns, 3 * GDN_CONV), state_gdn[0], w)
    x1_s, hf_s = _outproj(y_ssd_s, o_gdn_s, xs, mod_s, norm2_w[0], w_out_b, True, 1)

    tm = PEER_TM
    assert (nb * seq) % tm == 0 and ns <= tm
    hf_s = jnp.pad(hf_s, ((0, tm - ns), (0, 0)))
    scores_t = _peer_scores(hf_p, hf_s, wq_b, sk, tm)
    rank1, cnt, f0, e1 = _peer_select(scores_t, 256)
    peer_out = _peer_dense(hf_p, hf_s, u_b, v_b, rank1, cnt, f0, e1, tm, PEER_TE)

    y_p = _final(x1_p, peer_out, mod_p, final_norm_w, False, seq, 0)
    y_s = _final(x1_s, peer_out, mod_s, final_norm_w, True, 1, nb * seq)

    return (y_p.reshape(nb, seq, D), y_s.reshape(ns, 1, D),
            ssd_state_p[None], ssd_conv_p[None], gdn_state_p[None], gdn_conv_p[None],
            ssd_state_s[None], ssd_conv_s.reshape(1, ns, CONV_W - 1, SSD_CONV),
            gdn_state_s[None], gdn_conv_s.reshape(1, ns, CONV_W - 1, GDN_CONV))
```

```python
import functools
import math

import jax
import jax.numpy as jnp
from jax import lax
from jax.experimental import pallas as pl
from jax.experimental.pallas import tpu as pltpu

f32 = jnp.float32
bf16 = jnp.bfloat16

D = 2048
CONV_W = 4
SSD_P = 64
SSD_INNER = D // 2
SSD_H = SSD_INNER // SSD_P
SSD_G = 2
SSD_N = 128
SSD_CHUNK = 128
SSD_BC = 2 * SSD_G * SSD_N
SSD_CONV = SSD_INNER + SSD_BC
GDN_DK = 128
GDN_DV = 128
GDN_V = D // 2
GDN_H = GDN_V // GDN_DV
GDN_CHUNK = 64
GDN_CONV = GDN_H * (2 * GDN_DK + GDN_DV)
PEER_H = 8
PEER_NK = 128
PEER_E = PEER_NK * PEER_NK
PEER_DK = 256
PEER_K = 16
EPS = 1e-6
NEG = -1e30

OFF_Z, OFF_XS, OFF_GATE, OFF_QKV, OFF_BC, OFF_SM = 0, 1024, 2048, 3072, 6144, 6656
P_COLS = 7168
SM_DT, SM_A, SM_B = 0, 16, 24

VMEM_LIMIT = 56 * 1024 * 1024


def _cp(sem):
    return pltpu.CompilerParams(dimension_semantics=sem, vmem_limit_bytes=VMEM_LIMIT)


def _silu(x):
    return x * jax.nn.sigmoid(x)


def _softplus(x):
    return jnp.maximum(x, 0.0) + jnp.log1p(jnp.exp(-jnp.abs(x)))


def _dot(a, b):
    return jnp.dot(a, b, preferred_element_type=f32)


def _dot_nt(a, b):
    return lax.dot_general(a, b, (((1,), (1,)), ((), ())), preferred_element_type=f32)


def _dot_hi(a, b):
    return jnp.dot(a, b, preferred_element_type=f32, precision=lax.Precision.HIGHEST)


def _iota(shape, dim):
    return lax.broadcasted_iota(jnp.int32, shape, dim)


def _mod_kernel(c_ref, w_ref, b_ref, o_ref):
    a = _silu(c_ref[...]).astype(bf16)
    o_ref[...] = _dot(a, w_ref[...].astype(bf16)) + b_ref[...]


def _modulation(c_all, w_ada, b_ada):
    m, tn = c_all.shape[0], 1024
    return pl.pallas_call(
        _mod_kernel,
        grid=(6 * D // tn,),
        in_specs=[pl.BlockSpec((m, D), lambda j: (0, 0)),
                  pl.BlockSpec((D, tn), lambda j: (0, j)),
                  pl.BlockSpec((1, tn), lambda j: (0, j))],
        out_specs=pl.BlockSpec((m, tn), lambda j: (0, j)),
        out_shape=jax.ShapeDtypeStruct((m, 6 * D), f32),
        compiler_params=_cp(("arbitrary",)),
        name="adaln_mod",
    )(c_all, w_ada, b_ada.reshape(1, 6 * D))


def _mod_specs(per_token, tm, rows_per_seq, which, grid_rank):
    if per_token:
        if grid_rank == 2:
            return pl.BlockSpec((tm, D), lambda i, j: (i, which))
        return pl.BlockSpec((tm, D), lambda i: (i, which))
    tps = rows_per_seq // tm
    if grid_rank == 2:
        return pl.BlockSpec((None, 1, D), lambda i, j: (i // tps, 0, which))
    return pl.BlockSpec((None, 1, D), lambda i: (i // tps, 0, which))


def _inproj_kernel(x_ref, sh_ref, sc_ref, nw_ref, w_ref, o_ref, hm_ref):
    @pl.when(pl.program_id(1) == 0)
    def _():
        x = x_ref[...]
        y = x * lax.rsqrt(jnp.mean(x * x, axis=-1, keepdims=True) + EPS) * nw_ref[...]
        hm_ref[...] = (y * (1.0 + sc_ref[...]) + sh_ref[...]).astype(bf16)

    o_ref[...] = _dot(hm_ref[...], w_ref[...])


def _inproj(x, mod, norm_w, w_re, per_token, rows_per_seq):
    t = x.shape[0]
    tm = min(1024, t if per_token else rows_per_seq)
    assert t % tm == 0 and (per_token or rows_per_seq % tm == 0)
    tn = 1024
    return pl.pallas_call(
        _inproj_kernel,
        grid=(t // tm, P_COLS // tn),
        in_specs=[pl.BlockSpec((tm, D), lambda i, j: (i, 0)),
                  _mod_specs(per_token, tm, rows_per_seq, 0, 2),
                  _mod_specs(per_token, tm, rows_per_seq, 1, 2),
                  pl.BlockSpec((1, D), lambda i, j: (0, 0)),
                  pl.BlockSpec((D, tn), lambda i, j: (0, j))],
        out_specs=pl.BlockSpec((tm, tn), lambda i, j: (i, j)),
        out_shape=jax.ShapeDtypeStruct((t, P_COLS), f32),
        scratch_shapes=[pltpu.VMEM((tm, D), bf16)],
        compiler_params=_cp(("parallel", "arbitrary")),
        name="inproj",
    )(x, mod, mod, norm_w.reshape(1, D), w_re)


def _conv_chunk(ext_ref, cur, w_ref, b_row, q):
    ext_ref[8:8 + q, :] = cur
    acc = w_ref[CONV_W - 1:CONV_W, :] * cur
    for k in range(CONV_W - 1):
        acc = acc + w_ref[k:k + 1, :] * ext_ref[5 + k:5 + k + q, :]
    ext_ref[0:8, :] = cur[q - 8:q, :]
    if b_row is not None:
        acc = acc + b_row
    return acc


def _group_rmsnorm(y, nw, width):
    outs = []
    for g in range(y.shape[-1] // width):
        seg = y[:, g * width:(g + 1) * width]
        ms = jnp.mean(seg * seg, axis=-1, keepdims=True)
        outs.append(seg * lax.rsqrt(ms + EPS) * nw[:, g * width:(g + 1) * width])
    return jnp.concatenate(outs, axis=-1)


def _lane_mask(shape, lo, hi):
    lane = _iota(shape, len(shape) - 1)
    return (lane >= lo) & (lane < hi)


def _ssd_kernel(z_ref, xs_ref, bc_ref, sm_ref, wxs_ref, wbc_ref, bxs_ref, bbc_ref, dtb_ref, alog_ref, de_ref,
                nw_ref, y_ref, st_ref, extx, extb, ht, yscr):
    q = SSD_CHUNK
    c = pl.program_id(1)

    @pl.when(c == 0)
    def _():
        extx[0:8, :] = jnp.zeros((8, SSD_INNER), f32)
        extb[0:8, :] = jnp.zeros((8, SSD_BC), f32)
        ht[...] = jnp.zeros(ht.shape, f32)

    xs = _silu(_conv_chunk(extx, xs_ref[...], wxs_ref, bxs_ref[...], q))
    bcv = _silu(_conv_chunk(extb, bc_ref[...], wbc_ref, bbc_ref[...], q))

    head_lanes = _lane_mask((q, 128), SM_DT, SM_DT + SSD_H)
    dt = jnp.where(head_lanes, _softplus(sm_ref[...] + dtb_ref[...]), 0.0)
    a_row = -jnp.exp(alog_ref[...])
    tri = (_iota((q, q), 0) >= _iota((q, q), 1))
    cs = _dot_hi(tri.astype(f32), dt * a_row)
    cs_t = cs.T
    dt_t = dt.T
    last_col = cs_t[:, q - 1:q]
    w_t = dt_t * jnp.exp(last_col - cs_t)
    ecs = jnp.exp(cs)
    lane_lo = _iota((q, 128), 1) < SSD_P

    for g in range(SSD_G):
        b_g = bcv[:, g * SSD_N:(g + 1) * SSD_N]
        c_g = bcv[:, SSD_G * SSD_N + g * SSD_N:SSD_G * SSD_N + (g + 1) * SSD_N]
        b_gt = b_g.T
        cb = _dot(c_g.astype(bf16), b_gt.astype(bf16))
        for r2 in range(SSD_H // SSD_G // 2):
            pair = g * (SSD_H // SSD_G // 2) + r2
            xs_pair = xs[:, pair * 128:(pair + 1) * 128]
            xs_pair_b = xs_pair.astype(bf16)
            h_prev = ht[pair]
            rhs = jnp.concatenate([xs_pair_b, h_prev.astype(bf16)], axis=0)
            ys, hs, els = [], [], []
            for e in range(2):
                h = 2 * pair + e
                diff = cs[:, h:h + 1] - cs_t[h:h + 1, :]
                dec = jnp.exp(jnp.where(tri, diff, NEG))
                m = cb * dec * dt_t[h:h + 1, :]
                lhs = jnp.concatenate([m, c_g * ecs[:, h:h + 1]], axis=1).astype(bf16)
                ys.append(_dot(lhs, rhs))
                bw = (b_gt * w_t[h:h + 1, :]).astype(bf16)
                hs.append(_dot(bw, xs_pair_b))
                els.append(jnp.exp(last_col[h:h + 1, :]))
            yscr[:, pair * 128:(pair + 1) * 128] = jnp.where(lane_lo, ys[0], ys[1])
            ht[pair] = h_prev * jnp.where(lane_lo, els[0], els[1]) + jnp.where(lane_lo, hs[0], hs[1])

    z = z_ref[...]
    y = (yscr[...] + de_ref[...] * xs) * _silu(z)
    y_ref[...] = _group_rmsnorm(y, nw_ref[...], SSD_INNER // SSD_G).astype(y_ref.dtype)

    @pl.when(c == pl.num_programs(1) - 1)
    def _():
        for pair in range(SSD_H // 2):
            t = ht[pair].T
            st_ref[2 * pair] = t[0:SSD_P]
            st_ref[2 * pair + 1] = t[SSD_P:2 * SSD_P]


def _ssd_prompt(proj, nb, seq, w):
    nc = seq // SSD_CHUNK
    q = SSD_CHUNK
    row = lambda n, c: n * nc + c
    const = lambda shape: pl.BlockSpec(shape, lambda n, c: (0, 0))
    return pl.pallas_call(
        _ssd_kernel,
        grid=(nb, nc),
        in_specs=[pl.BlockSpec((q, 1024), lambda n, c: (row(n, c), OFF_Z // 1024)),
                  pl.BlockSpec((q, 1024), lambda n, c: (row(n, c), OFF_XS // 1024)),
                  pl.BlockSpec((q, SSD_BC), lambda n, c: (row(n, c), OFF_BC // SSD_BC)),
                  pl.BlockSpec((q, 128), lambda n, c: (row(n, c), OFF_SM // 128)),
                  const((CONV_W, SSD_INNER)), const((CONV_W, SSD_BC)), const((1, SSD_INNER)), const((1, SSD_BC)),
                  const((1, 128)), const((1, 128)), const((1, SSD_INNER)), const((1, SSD_INNER))],
        out_specs=[pl.BlockSpec((q, SSD_INNER), lambda n, c: (row(n, c), 0)),
                   pl.BlockSpec((None, SSD_H, SSD_P, SSD_N), lambda n, c: (n, 0, 0, 0))],
        out_shape=[jax.ShapeDtypeStruct((nb * seq, SSD_INNER), bf16),
                   jax.ShapeDtypeStruct((nb, SSD_H, SSD_P, SSD_N), f32)],
        scratch_shapes=[pltpu.VMEM((q + 8, SSD_INNER), f32), pltpu.VMEM((q + 8, SSD_BC), f32),
                        pltpu.VMEM((SSD_H // 2, SSD_N, 2 * SSD_P), f32), pltpu.VMEM((q, SSD_INNER), f32)],
        compiler_params=_cp(("parallel", "arbitrary")),
        name="ssd_scan",
    )(proj, proj, proj, proj, w["ssd_wxs"], w["ssd_wbc"], w["ssd_bxs"], w["ssd_bbc"], w["ssd_dtb"],
      w["ssd_alog"], w["ssd_de"], w["ssd_nw"])


def _gdn_kernel(qkv_ref, gate_ref, sm_ref, cw_ref, dtb_ref, alog_ref, nw_ref, o_ref, st_ref, ext, s_scr):
    q = GDN_CHUNK
    n_seq = qkv_ref.shape[0]
    c = pl.program_id(1)

    @pl.when(c == 0)
    def _():
        for s in range(n_seq):
            ext[s, 0:8, :] = jnp.zeros((8, GDN_CONV), f32)
        s_scr[...] = jnp.zeros(s_scr.shape, f32)

    incl = _iota((q, q), 0) >= _iota((q, q), 1)
    strict = _iota((q, q), 0) > _iota((q, q), 1)
    rb, cb_ = _iota((q, q), 0), _iota((q, q), 1)
    same16 = (rb // 16) == (cb_ // 16)
    same32 = (rb // 32) == (cb_ // 32)
    diag_mask = strict & same16
    off32_mask = strict & same32 & jnp.logical_not(same16)
    off64_mask = strict & jnp.logical_not(same32)
    a_lanes = _lane_mask((q, 128), SM_A, SM_A + GDN_H)
    nw = nw_ref[...]

    heads = range(n_seq * GDN_H)
    qn, kn, kb, dec, ecol, bcol, gcol, vh = [], [], [], [], [], [], [], []
    for s in range(n_seq):
        act = _silu(_conv_chunk(ext.at[s], qkv_ref[s], cw_ref, None, q))
        sm = sm_ref[s]
        g = jnp.where(a_lanes, -jnp.exp(alog_ref[...]) * _softplus(sm + dtb_ref[...]), 0.0)
        beta = jax.nn.sigmoid(sm)
        gc = _dot_hi(incl.astype(f32), g)
        gc_t = gc.T
        egc = jnp.exp(gc)
        for h in range(GDN_H):
            qh = act[:, h * GDN_DK:(h + 1) * GDN_DK]
            kh = act[:, GDN_H * GDN_DK + h * GDN_DK:GDN_H * GDN_DK + (h + 1) * GDN_DK]
            vh.append(act[:, 2 * GDN_H * GDN_DK + h * GDN_DV:2 * GDN_H * GDN_DK + (h + 1) * GDN_DV])
            qn.append(qh * lax.rsqrt(jnp.sum(qh * qh, axis=-1, keepdims=True) + EPS) * (GDN_DK ** -0.5))
            kn.append(kh * lax.rsqrt(jnp.sum(kh * kh, axis=-1, keepdims=True) + EPS))
            gcol.append(gc[:, SM_A + h:SM_A + h + 1])
            grow = gc_t[SM_A + h:SM_A + h + 1, :]
            ecol.append(egc[:, SM_A + h:SM_A + h + 1])
            bcol.append(beta[:, SM_B + h:SM_B + h + 1])
            dec.append(jnp.exp(jnp.where(incl, gcol[-1] - grow, NEG)))
            kb.append(kn[-1] * bcol[-1])
    kn_b = [kn[h].astype(bf16) for h in heads]
    lmat = [_dot_nt(kb[h].astype(bf16), kn_b[h]) * dec[h] for h in heads]
    qk = [_dot_nt(qn[h].astype(bf16), kn_b[h]) * dec[h] for h in heads]

    dotb = lambda a, b: _dot(a.astype(bf16), b.astype(bf16))
    n_pow = [jnp.where(diag_mask, -lmat[h], 0.0) for h in heads]
    t_m = list(n_pow)
    for _ in range(3):
        n_pow = [dotb(n_pow[h], n_pow[h]) for h in heads]
        t_m = [t_m[h] + n_pow[h] + dotb(t_m[h], n_pow[h]) for h in heads]
    for mask in (off32_mask, off64_mask):
        off = [jnp.where(mask, lmat[h], 0.0) for h in heads]
        y = [off[h] + dotb(off[h], t_m[h]) for h in heads]
        t_m = [t_m[h] - (y[h] + dotb(t_m[h], y[h])) for h in heads]

    rhs = [jnp.concatenate([vh[h] * bcol[h], kb[h] * ecol[h]], axis=1) for h in heads]
    sol = [rhs[h] + dotb(t_m[h], rhs[h]) for h in heads]
    s_prev = [s_scr[h] for h in heads]
    s_b = [s_prev[h].astype(bf16) for h in heads]
    v_new = [sol[h][:, :GDN_DV] - _dot(sol[h][:, GDN_DV:].astype(bf16), s_b[h]) for h in heads]
    v_new_b = [v_new[h].astype(bf16) for h in heads]
    o = [_dot((qn[h] * ecol[h]).astype(bf16), s_b[h]) + _dot(qk[h].astype(bf16), v_new_b[h]) for h in heads]
    for h in heads:
        glast = gcol[h][q - 1:q, :]
        kdec = kn[h] * jnp.exp(glast - gcol[h])
        s_scr[h] = s_prev[h] * jnp.exp(glast) + _dot(kdec.T.astype(bf16), v_new_b[h])
    for s in range(n_seq):
        gate = gate_ref[s]
        for h in range(GDN_H):
            oh = o[s * GDN_H + h]
            on = oh * lax.rsqrt(jnp.mean(oh * oh, axis=-1, keepdims=True) + EPS) * nw
            o_ref[s, :, h * GDN_DV:(h + 1) * GDN_DV] = (
                on * _silu(gate[:, h * GDN_DV:(h + 1) * GDN_DV])).astype(o_ref.dtype)

    @pl.when(c == pl.num_programs(1) - 1)
    def _():
        for s in range(n_seq):
            st_ref[s] = s_scr[s * GDN_H:(s + 1) * GDN_H]


def _gdn_prompt(proj, nb, seq, w):
    nc = seq // GDN_CHUNK
    q = GDN_CHUNK
    n_seq = 4 if nb % 4 == 0 else (2 if nb % 2 == 0 else 1)
    proj3 = proj.reshape(nb, seq, P_COLS)
    const = lambda shape: pl.BlockSpec(shape, lambda n, c: (0, 0))
    o, st = pl.pallas_call(
        _gdn_kernel,
        grid=(nb // n_seq, nc),
        in_specs=[pl.BlockSpec((n_seq, q, GDN_CONV), lambda n, c: (n, c, OFF_QKV // GDN_CONV)),
                  pl.BlockSpec((n_seq, q, GDN_V), lambda n, c: (n, c, OFF_GATE // GDN_V)),
                  pl.BlockSpec((n_seq, q, 128), lambda n, c: (n, c, OFF_SM // 128)),
                  const((CONV_W, GDN_CONV)), const((1, 128)), const((1, 128)), const((1, GDN_DV))],
        out_specs=[pl.BlockSpec((n_seq, q, GDN_V), lambda n, c: (n, c, 0)),
                   pl.BlockSpec((n_seq, GDN_H, GDN_DK, GDN_DV), lambda n, c: (n, 0, 0, 0))],
        out_shape=[jax.ShapeDtypeStruct((nb, seq, GDN_V), bf16),
                   jax.ShapeDtypeStruct((nb, GDN_H, GDN_DK, GDN_DV), f32)],
        scratch_shapes=[pltpu.VMEM((n_seq, q + 8, GDN_CONV), f32),
                        pltpu.VMEM((n_seq * GDN_H, GDN_DK, GDN_DV), f32)],
        compiler_params=_cp(("parallel", "arbitrary")),
        name="gdn_scan",
    )(proj3, proj3, proj3, w["gdn_cw"], w["gdn_dtb"], w["gdn_alog"], w["gdn_nw"])
    return o.reshape(nb * seq, GDN_V), st


def _conv_step(buf, xraw, w_ref, width):
    acc = w_ref[CONV_W - 1:CONV_W, :] * xraw
    for k in range(CONV_W - 1):
        acc = acc + w_ref[k:k + 1, :] * buf[:, k * width:(k + 1) * width]
    return acc, jnp.concatenate([buf[:, width:], xraw], axis=1)


def _ssd_step_kernel(z_ref, xs_ref, bc_ref, sm_ref, buf_ref, st_ref, cw_ref, cb_ref, dtb_ref, alog_ref, de_ref,
                     nw_ref, ex_ref, y_ref, nbuf_ref, nst_ref, yscr):
    bs = z_ref.shape[0]
    xraw = jnp.concatenate([xs_ref[...], bc_ref[...]], axis=1)
    conv, nbuf = _conv_step(buf_ref[...], xraw, cw_ref, SSD_CONV)
    nbuf_ref[...] = nbuf
    act = _silu(conv + cb_ref[...])
    xs = act[:, :SSD_INNER]
    head_lanes = _lane_mask((bs, 128), SM_DT, SM_DT + SSD_H)
    dt = jnp.where(head_lanes, _softplus(sm_ref[...] + dtb_ref[...]), 0.0)
    da = jnp.exp(dt * (-jnp.exp(alog_ref[...])))
    dt_e = _dot_hi(dt, ex_ref[...])
    da_e = _dot_hi(da, ex_ref[...])
    cols = jnp.concatenate([xs * dt_e, da_e], axis=0).T
    gw = SSD_INNER // SSD_G
    hg = SSD_H // SSD_G
    for b in range(bs):
        xcol = cols[:, b:b + 1]
        dcol = cols[:, bs + b:bs + b + 1]
        for g in range(SSD_G):
            b_row = act[b:b + 1, SSD_INNER + g * SSD_N:SSD_INNER + (g + 1) * SSD_N]
            c_row = act[b:b + 1, SSD_INNER + (SSD_G + g) * SSD_N:SSD_INNER + (SSD_G + g + 1) * SSD_N]
            st = st_ref[b, g * hg:(g + 1) * hg].reshape(gw, SSD_N)
            new = st * dcol[g * gw:(g + 1) * gw] + xcol[g * gw:(g + 1) * gw] * b_row
            nst_ref[b, g * hg:(g + 1) * hg] = new.reshape(hg, SSD_P, SSD_N)
            c8 = jnp.broadcast_to(c_row, (8, SSD_N))
            yscr[b:b + 1, g * gw:(g + 1) * gw] = _dot_nt(c8, new)[0:1]
    y = (yscr[...] + de_ref[...] * xs) * _silu(z_ref[...])
    y_ref[...] = _group_rmsnorm(y, nw_ref[...], gw).astype(y_ref.dtype)


def _ssd_step(proj, buf, state, w, bs=8):
    n = proj.shape[0]
    const = lambda shape: pl.BlockSpec(shape, lambda i: (0,) * len(shape))
    return pl.pallas_call(
        _ssd_step_kernel,
        grid=(n // bs,),
        in_specs=[pl.BlockSpec((bs, 1024), lambda i: (i, OFF_Z // 1024)),
                  pl.BlockSpec((bs, 1024), lambda i: (i, OFF_XS // 1024)),
                  pl.BlockSpec((bs, SSD_BC), lambda i: (i, OFF_BC // SSD_BC)),
                  pl.BlockSpec((bs, 128), lambda i: (i, OFF_SM // 128)),
                  pl.BlockSpec((bs, 3 * SSD_CONV), lambda i: (i, 0)),
                  pl.BlockSpec((bs, SSD_H, SSD_P, SSD_N), lambda i: (i, 0, 0, 0)),
                  const((CONV_W, SSD_CONV)), const((1, SSD_CONV)), const((1, 128)), const((1, 128)),
                  const((1, SSD_INNER)), const((1, SSD_INNER)), const((128, SSD_INNER))],
        out_specs=[pl.BlockSpec((bs, SSD_INNER), lambda i: (i, 0)),
                   pl.BlockSpec((bs, 3 * SSD_CONV), lambda i: (i, 0)),
                   pl.BlockSpec((bs, SSD_H, SSD_P, SSD_N), lambda i: (i, 0, 0, 0))],
        out_shape=[jax.ShapeDtypeStruct((n, SSD_INNER), bf16),
                   jax.ShapeDtypeStruct((n, 3 * SSD_CONV), f32),
                   jax.ShapeDtypeStruct((n, SSD_H, SSD_P, SSD_N), f32)],
        scratch_shapes=[pltpu.VMEM((bs, SSD_INNER), f32)],
        compiler_params=_cp(("parallel",)),
        name="ssd_step",
    )(proj, proj, proj, proj, buf, state, w["ssd_cw"], w["ssd_cb"], w["ssd_dtb"], w["ssd_alog"], w["ssd_de"],
      w["ssd_nw"], w["ssd_expand"])


def _gdn_step_kernel(qkv_ref, gate_ref, sm_ref, buf_ref, st_ref, cw_ref, dtb_ref, alog_ref, nw_ref,
                     o_ref, nbuf_ref, nst_ref, oscr):
    bs = qkv_ref.shape[0]
    conv, nbuf = _conv_step(buf_ref[...], qkv_ref[...], cw_ref, GDN_CONV)
    nbuf_ref[...] = nbuf
    act = _silu(conv)
    sm = sm_ref[...]
    eg = jnp.exp(-jnp.exp(alog_ref[...]) * _softplus(sm + dtb_ref[...]))
    beta = jax.nn.sigmoid(sm)
    qs, ks = [], []
    for h in range(GDN_H):
        qh = act[:, h * GDN_DK:(h + 1) * GDN_DK]
        kh = act[:, GDN_H * GDN_DK + h * GDN_DK:GDN_H * GDN_DK + (h + 1) * GDN_DK]
        qs.append(qh * lax.rsqrt(jnp.sum(qh * qh, axis=-1, keepdims=True) + EPS) * (GDN_DK ** -0.5))
        ks.append(kh * lax.rsqrt(jnp.sum(kh * kh, axis=-1, keepdims=True) + EPS))
    qk_t = jnp.concatenate(qs + ks, axis=1).T
    for b in range(bs):
        for h in range(GDN_H):
            qcol = qk_t[h * GDN_DK:(h + 1) * GDN_DK, b:b + 1]
            kcol = qk_t[(GDN_H + h) * GDN_DK:(GDN_H + h + 1) * GDN_DK, b:b + 1]
            vrow = act[b:b + 1, 2 * GDN_H * GDN_DK + h * GDN_DV:2 * GDN_H * GDN_DK + (h + 1) * GDN_DV]
            egs = eg[b:b + 1, SM_A + h:SM_A + h + 1]
            bet = beta[b:b + 1, SM_B + h:SM_B + h + 1]
            s = st_ref[b, h]
            v_new = bet * (vrow - egs * jnp.sum(kcol * s, axis=0, keepdims=True))
            s_new = s * egs + kcol * v_new
            nst_ref[b, h] = s_new
            oscr[b:b + 1, h * GDN_DV:(h + 1) * GDN_DV] = jnp.sum(qcol * s_new, axis=0, keepdims=True)
    o = oscr[...]
    nw = nw_ref[...]
    gate = gate_ref[...]
    for h in range(GDN_H):
        oh = o[:, h * GDN_DV:(h + 1) * GDN_DV]
        on = oh * lax.rsqrt(jnp.mean(oh * oh, axis=-1, keepdims=True) + EPS) * nw
        o_ref[:, h * GDN_DV:(h + 1) * GDN_DV] = (on * _silu(gate[:, h * GDN_DV:(h + 1) * GDN_DV])).astype(o_ref.dtype)


def _gdn_step(proj, buf, state, w, bs=8):
    n = proj.shape[0]
    const = lambda shape: pl.BlockSpec(shape, lambda i: (0,) * len(shape))
    return pl.pallas_call(
        _gdn_step_kernel,
        grid=(n // bs,),
        in_specs=[pl.BlockSpec((bs, GDN_CONV), lambda i: (i, OFF_QKV // GDN_CONV)),
                  pl.BlockSpec((bs, GDN_V), lambda i: (i, OFF_GATE // GDN_V)),
                  pl.BlockSpec((bs, 128), lambda i: (i, OFF_SM // 128)),
                  pl.BlockSpec((bs, 3 * GDN_CONV), lambda i: (i, 0)),
                  pl.BlockSpec((bs, GDN_H, GDN_DK, GDN_DV), lambda i: (i, 0, 0, 0)),
                  const((CONV_W, GDN_CONV)), const((1, 128)), const((1, 128)), const((1, GDN_DV))],
        out_specs=[pl.BlockSpec((bs, GDN_V), lambda i: (i, 0)),
                   pl.BlockSpec((bs, 3 * GDN_CONV), lambda i: (i, 0)),
                   pl.BlockSpec((bs, GDN_H, GDN_DK, GDN_DV), lambda i: (i, 0, 0, 0))],
        out_shape=[jax.ShapeDtypeStruct((n, GDN_V), bf16),
                   jax.ShapeDtypeStruct((n, 3 * GDN_CONV), f32),
                   jax.ShapeDtypeStruct((n, GDN_H, GDN_DK, GDN_DV), f32)],
        scratch_shapes=[pltpu.VMEM((bs, GDN_V), f32)],
        compiler_params=_cp(("parallel",)),
        name="gdn_step",
    )(proj, proj, proj, buf, state, w["gdn_cw"], w["gdn_dtb"], w["gdn_alog"], w["gdn_nw"])


def _outproj_kernel(ys_ref, og_ref, x_ref, g1_ref, sh_ref, sc_ref, nw_ref, w_ref, x1_ref, hf_ref):
    m = _dot(ys_ref[...], w_ref[0:SSD_INNER, :]) + _dot(og_ref[...], w_ref[SSD_INNER:, :])
    x1 = x_ref[...] + g1_ref[...] * m
    x1_ref[...] = x1
    y = x1 * lax.rsqrt(jnp.mean(x1 * x1, axis=-1, keepdims=True) + EPS) * nw_ref[...]
    hf_ref[...] = (y * (1.0 + sc_ref[...]) + sh_ref[...]).astype(hf_ref.dtype)


def _outproj(ys, og, x, mod, norm_w, w_out_b, per_token, rows_per_seq):
    t = x.shape[0]
    tm = min(512, t)
    return pl.pallas_call(
        _outproj_kernel,
        grid=(t // tm,),
        in_specs=[pl.BlockSpec((tm, SSD_INNER), lambda i: (i, 0)),
                  pl.BlockSpec((tm, GDN_V), lambda i: (i, 0)),
                  pl.BlockSpec((tm, D), lambda i: (i, 0)),
                  _mod_specs(per_token, tm, rows_per_seq, 2, 1),
                  _mod_specs(per_token, tm, rows_per_seq, 3, 1),
                  _mod_specs(per_token, tm, rows_per_seq, 4, 1),
                  pl.BlockSpec((1, D), lambda i: (0, 0)),
                  pl.BlockSpec((D, D), lambda i: (0, 0))],
        out_specs=[pl.BlockSpec((tm, D), lambda i: (i, 0)), pl.BlockSpec((tm, D), lambda i: (i, 0))],
        out_shape=[jax.ShapeDtypeStruct((t, D), f32), jax.ShapeDtypeStruct((t, D), bf16)],
        compiler_params=_cp(("parallel",)),
        name="outproj",
    )(ys, og, x, mod, mod, mod, norm_w.reshape(1, D), w_out_b)


def _pick_token_tile(i, n_main, main_ref, tail_ref, dst_ref):
    @pl.when(i < n_main)
    def _():
        dst_ref[...] = main_ref[...]

    @pl.when(i >= n_main)
    def _():
        dst_ref[...] = tail_ref[...]


def _peer_scores_kernel(hfp_ref, hfs_ref, wq_ref, sk_ref, o_ref, hf_scr):
    _pick_token_tile(pl.program_id(0), pl.num_programs(0) - 1, hfp_ref, hfs_ref, hf_scr)
    q_t = _dot_nt(wq_ref[...], hf_scr[...])
    half = PEER_DK // 2
    for hs in range(2 * PEER_H):
        o_ref[hs] = _dot(sk_ref[hs], q_t[hs * half:(hs + 1) * half].astype(bf16))


def _peer_scores(hf_p, hf_s, wq_b, sk, tm):
    n_main = hf_p.shape[0] // tm
    t = (n_main + 1) * tm
    return pl.pallas_call(
        _peer_scores_kernel,
        grid=(n_main + 1,),
        in_specs=[pl.BlockSpec((tm, D), lambda i: (jnp.minimum(i, n_main - 1), 0)),
                  pl.BlockSpec((tm, D), lambda i: (0, 0)),
                  pl.BlockSpec((PEER_H * PEER_DK, D), lambda i: (0, 0)),
                  pl.BlockSpec((2 * PEER_H, PEER_NK, PEER_DK // 2), lambda i: (0, 0, 0))],
        out_specs=pl.BlockSpec((2 * PEER_H, PEER_NK, tm), lambda i: (0, 0, i)),
        out_shape=jax.ShapeDtypeStruct((2 * PEER_H, PEER_NK, t), f32),
        scratch_shapes=[pltpu.VMEM((tm, D), bf16)],
        compiler_params=_cp(("arbitrary",)),
        name="peer_scores",
    )(hf_p, hf_s, wq_b, sk)


_STAIR = [PEER_K // (k1 + 1) for k1 in range(PEER_K)]
_STAIR_ROWS = -(-sum(_STAIR) // 8) * 8


def _peer_select_kernel(s_ref, rank_ref, cnt_ref, f0_ref, e1_ref, v1_scr, cand):
    tm = s_ref.shape[-1]
    for h in range(PEER_H):
        s0 = s_ref[2 * h]
        s1 = s_ref[2 * h + 1]
        tops = []
        rank1 = jnp.full((PEER_NK, tm), float(PEER_K), f32)
        for side, cur in enumerate((s0, s1)):
            vals = []
            for k in range(PEER_K):
                m = jnp.max(cur, axis=0, keepdims=True)
                vals.append(m)
                hit = cur == m
                if side == 1:
                    rank1 = jnp.where(hit, float(k), rank1)
                cur = jnp.where(hit, -jnp.inf, cur)
            tops.append(vals)
        for k in range(PEER_K):
            v1_scr[k:k + 1, :] = tops[1][k]
        cand[_STAIR_ROWS - 8:_STAIR_ROWS, :] = jnp.full((8, tm), -jnp.inf, f32)
        off = 0
        for k1 in range(PEER_K):
            cand[off:off + _STAIR[k1], :] = tops[0][k1] + v1_scr[0:_STAIR[k1], :]
            off += _STAIR[k1]
        cv = cand[...]
        cur = cv
        for _ in range(PEER_K):
            tau = jnp.max(cur, axis=0, keepdims=True)
            cur = jnp.where(cur == tau, -jnp.inf, cur)
        m0, m1 = tops[0][0], tops[1][0]
        zsum = jnp.sum(jnp.where(cv >= tau, jnp.exp(cv - (m0 + m1)), 0.0), axis=0, keepdims=True)
        cnt = jnp.zeros((PEER_NK, tm), f32)
        for k1 in range(PEER_K):
            v1 = v1_scr[0:_STAIR[k1], :]
            n_k = jnp.sum(jnp.where(tops[0][k1] + v1 >= tau, 1.0, 0.0), axis=0, keepdims=True)
            cnt = jnp.where(s0 == tops[0][k1], n_k, cnt)
        rank_ref[h] = rank1.astype(bf16)
        cnt_ref[h] = cnt
        f0_ref[h] = jnp.exp(s0 - m0) / zsum
        e1_ref[h] = jnp.exp(s1 - m1).astype(bf16)


def _peer_select(scores_t, tm):
    t = scores_t.shape[-1]
    spec = pl.BlockSpec((PEER_H, PEER_NK, tm), lambda i: (0, 0, i))
    shape = lambda dt: jax.ShapeDtypeStruct((PEER_H, PEER_NK, t), dt)
    return pl.pallas_call(
        _peer_select_kernel,
        grid=(t // tm,),
        in_specs=[pl.BlockSpec((2 * PEER_H, PEER_NK, tm), lambda i: (0, 0, i))],
        out_specs=[spec, spec, spec, spec],
        out_shape=[shape(bf16), shape(f32), shape(f32), shape(bf16)],
        scratch_shapes=[pltpu.VMEM((PEER_K, tm), f32), pltpu.VMEM((_STAIR_ROWS, tm), f32)],
        compiler_params=_cp(("parallel",)),
        name="peer_select",
    )(scores_t)


def _gelu(x):
    return 0.5 * x * (1.0 + lax.erf(x * (1.0 / math.sqrt(2.0))))


def _peer_dense_kernel(hfp_ref, hfs_ref, u_ref, v_ref, rank_ref, cnt_ref, f0_ref, e1_ref, o_ref, cd, hf_scr):
    tm = hf_scr.shape[0]
    te = u_ref.shape[0]
    sub = PEER_SUB_ROWS

    @pl.when(pl.program_id(1) == 0)
    def _():
        o_ref[...] = jnp.zeros(o_ref.shape, f32)
        _pick_token_tile(pl.program_id(0), pl.num_programs(0) - 1, hfp_ref, hfs_ref, hf_scr)

    a_t = _dot_nt(u_ref[...], hf_scr[...])
    for i1 in range(te // PEER_NK):
        for c in range(0, tm, 128):
            lanes = slice(c, c + 128)
            cnt_b = [jnp.broadcast_to(cnt_ref[h, i1:i1 + 1, lanes], (sub, 128)).astype(bf16) for h in range(PEER_H)]
            f0_b = [jnp.broadcast_to(f0_ref[h, i1:i1 + 1, lanes], (sub, 128)).astype(bf16) for h in range(PEER_H)]
            for r in range(0, PEER_NK, sub):
                wb = None
                for h in range(PEER_H):
                    keep = rank_ref[h, r:r + sub, lanes] < cnt_b[h]
                    term = jnp.where(keep, e1_ref[h, r:r + sub, lanes], jnp.zeros((), bf16)) * f0_b[h]
                    wb = term if wb is None else wb + term
                rows = slice(i1 * PEER_NK + r, i1 * PEER_NK + r + sub)
                cd[rows, lanes] = wb * _gelu(a_t[rows, lanes]).astype(bf16)
    o_ref[...] += lax.dot_general(cd[...], v_ref[...], (((0,), (0,)), ((), ())), preferred_element_type=f32)


def _peer_dense(hf_p, hf_s, u_b, v_b, rank1, cnt, f0, e1, tm, te):
    n_main = hf_p.shape[0] // tm
    t = (n_main + 1) * tm
    ni1 = te // PEER_NK
    assert ni1 == 8, "one 8-row block of side-0 rows per expert tile"
    rows8 = pl.BlockSpec((PEER_H, ni1, tm), lambda i, j: (0, j, i))
    full = pl.BlockSpec((PEER_H, PEER_NK, tm), lambda i, j: (0, 0, i))
    return pl.pallas_call(
        _peer_dense_kernel,
        grid=(n_main + 1, PEER_E // te),
        in_specs=[pl.BlockSpec((tm, D), lambda i, j: (jnp.minimum(i, n_main - 1), 0)),
                  pl.BlockSpec((tm, D), lambda i, j: (0, 0)),
                  pl.BlockSpec((te, D), lambda i, j: (j, 0)),
                  pl.BlockSpec((te, D), lambda i, j: (j, 0)),
                  full, rows8, rows8, full],
        out_specs=pl.BlockSpec((tm, D), lambda i, j: (i, 0)),
        out_shape=jax.ShapeDtypeStruct((t, D), f32),
        scratch_shapes=[pltpu.VMEM((te, tm), bf16), pltpu.VMEM((tm, D), bf16)],
        compiler_params=_cp(("arbitrary", "arbitrary")),
        name="peer_dense",
    )(hf_p, hf_s, u_b, v_b, rank1, cnt, f0, e1)


def _final_kernel(x1_ref, p_ref, g2_ref, nw_ref, o_ref):
    x2 = x1_ref[...] + g2_ref[...] * p_ref[...]
    o_ref[...] = x2 * lax.rsqrt(jnp.mean(x2 * x2, axis=-1, keepdims=True) + EPS) * nw_ref[...]


def _final(x1, peer_out, mod, norm_w, per_token, rows_per_seq, row_off):
    t = x1.shape[0]
    tm = min(512, t)
    off = row_off // tm
    return pl.pallas_call(
        _final_kernel,
        grid=(t // tm,),
        in_specs=[pl.BlockSpec((tm, D), lambda i: (i, 0)),
                  pl.BlockSpec((tm, D), lambda i: (i + off, 0)),
                  _mod_specs(per_token, tm, rows_per_seq, 5, 1),
                  pl.BlockSpec((1, D), lambda i: (0, 0))],
        out_specs=pl.BlockSpec((tm, D), lambda i: (i, 0)),
        out_shape=jax.ShapeDtypeStruct((t, D), f32),
        compiler_params=_cp(("parallel",)),
        name="final_norm",
    )(x1, peer_out, mod, norm_w.reshape(1, D))


def _pad_row(v, lo):
    return jnp.zeros((1, 128), f32).at[0, lo:lo + v.shape[0]].set(v.astype(f32))


def _prep_weights(w_in, ssd_conv_w, ssd_conv_b, ssd_dt_bias, ssd_A_log, ssd_D, ssd_norm_w, gdn_conv_w,
                  gdn_dt_bias, gdn_A_log, gdn_norm_w):
    o_z, o_xbc = 0, SSD_INNER
    o_dt = o_xbc + SSD_CONV
    o_qkv = o_dt + SSD_H
    o_a = o_qkv + GDN_CONV
    o_b = o_a + GDN_H
    o_gate = o_b + GDN_H
    cols = [w_in[:, o_z:o_z + SSD_INNER], w_in[:, o_xbc:o_xbc + SSD_INNER], w_in[:, o_gate:o_gate + GDN_V],
            w_in[:, o_qkv:o_qkv + GDN_CONV], w_in[:, o_xbc + SSD_INNER:o_xbc + SSD_CONV],
            w_in[:, o_dt:o_dt + SSD_H], w_in[:, o_a:o_a + GDN_H], w_in[:, o_b:o_b + GDN_H]]
    used = sum(c.shape[1] for c in cols)
    w_re = jnp.concatenate([c.astype(bf16) for c in cols] + [jnp.zeros((D, P_COLS - used), bf16)], axis=1)
    expand = (jnp.arange(128)[:, None] == (jnp.arange(SSD_INNER)[None, :] // SSD_P)).astype(f32)
    return dict(
        w_in_re=w_re,
        ssd_wxs=ssd_conv_w[:, :SSD_INNER], ssd_wbc=ssd_conv_w[:, SSD_INNER:],
        ssd_bxs=ssd_conv_b[None, :SSD_INNER], ssd_bbc=ssd_conv_b[None, SSD_INNER:],
        ssd_cw=ssd_conv_w, ssd_cb=ssd_conv_b[None, :],
        ssd_dtb=_pad_row(ssd_dt_bias, SM_DT), ssd_alog=_pad_row(ssd_A_log, SM_DT),
        ssd_de=jnp.repeat(ssd_D.astype(f32), SSD_P)[None, :], ssd_nw=ssd_norm_w[None, :].astype(f32),
        ssd_expand=expand,
        gdn_cw=gdn_conv_w, gdn_dtb=_pad_row(gdn_dt_bias, SM_A), gdn_alog=_pad_row(gdn_A_log, SM_A),
        gdn_nw=gdn_norm_w[None, :].astype(f32),
    )


PEER_TM = 512
PEER_TE = 1024
PEER_SUB_ROWS = 16


def kernel(x_prompt, x_sample, c_prompt, c_sample, state_ssd, state_ssd_conv, state_gdn, state_gdn_conv, w_ada, b_ada, norm1_w, norm2_w, w_in, ssd_conv_w, ssd_conv_b, ssd_dt_bias, ssd_A_log, ssd_D, ssd_norm_w, gdn_conv_w, gdn_dt_bias, gdn_A_log, gdn_norm_w, w_out, peer_w_q, peer_sub_keys, peer_u, peer_v, final_norm_w):
    assert w_ada.shape[0] == 1, "single layer"
    nb, seq, _ = x_prompt.shape
    ns = x_sample.shape[0]
    assert x_sample.shape[1] == 1 and seq % SSD_CHUNK == 0 and seq % 512 == 0 and ns % 8 == 0
    w = _prep_weights(w_in[0], ssd_conv_w[0], ssd_conv_b[0], ssd_dt_bias[0], ssd_A_log[0], ssd_D[0], ssd_norm_w[0],
                      gdn_conv_w[0], gdn_dt_bias[0], gdn_A_log[0], gdn_norm_w[0])
    w_out_b = w_out[0].astype(bf16)
    wq_b = peer_w_q[0].T.astype(bf16)
    sk = peer_sub_keys[0].reshape(2 * PEER_H, PEER_NK, PEER_DK // 2).astype(bf16)
    u_b = peer_u[0].astype(bf16)
    v_b = peer_v[0].astype(bf16)

    nbp = -(-nb // 8) * 8
    c_all = jnp.concatenate([jnp.pad(c_prompt, ((0, nbp - nb), (0, 0))), c_sample], axis=0)
    mod = _modulation(c_all, w_ada[0], b_ada[0])
    mod_p = mod[:nb].reshape(nb, 1, 6 * D)
    mod_s = mod[nbp:]

    xp = x_prompt.reshape(nb * seq, D)
    xs = x_sample.reshape(ns, D)

    proj_p = _inproj(xp, mod_p, norm1_w[0], w["w_in_re"], False, seq)
    y_ssd_p, ssd_state_p = _ssd_prompt(proj_p, nb, seq, w)
    o_gdn_p, gdn_state_p = _gdn_prompt(proj_p, nb, seq, w)
    x1_p, hf_p = _outproj(y_ssd_p, o_gdn_p, xp, mod_p, norm2_w[0], w_out_b, False, seq)
    tail = proj_p.reshape(nb, seq, P_COLS)[:, seq - (CONV_W - 1):, :]
    ssd_conv_p = jnp.concatenate([tail[..., OFF_XS:OFF_XS + SSD_INNER], tail[..., OFF_BC:OFF_BC + SSD_BC]], axis=-1)
    gdn_conv_p = tail[..., OFF_QKV:OFF_QKV + GDN_CONV]

    proj_s = _inproj(xs, mod_s, norm1_w[0], w["w_in_re"], True, 1)
    y_ssd_s, ssd_conv_s, ssd_state_s = _ssd_step(proj_s, state_ssd_conv[0].reshape(ns, 3 * SSD_CONV), state_ssd[0], w)
    o_gdn_s, gdn_conv_s, gdn_state_s = _gdn_step(proj_s, state_gdn_conv[0].reshape(ns, 3 * GDN_CONV), state_gdn[0], w)
    x1_s, hf_s = _outproj(y_ssd_s, o_gdn_s, xs, mod_s, norm2_w[0], w_out_b, True, 1)

    tm = PEER_TM
    assert (nb * seq) % tm == 0 and ns <= tm
    hf_s = jnp.pad(hf_s, ((0, tm - ns), (0, 0)))
    scores_t = _peer_scores(hf_p, hf_s, wq_b, sk, tm)
    rank1, cnt, f0, e1 = _peer_select(scores_t, 256)
    peer_out = _peer_dense(hf_p, hf_s, u_b, v_b, rank1, cnt, f0, e1, tm, PEER_TE)

    y_p = _final(x1_p, peer_out, mod_p, final_norm_w, False, seq, 0)
    y_s = _final(x1_s, peer_out, mod_s, final_norm_w, True, 1, nb * seq)

    return (y_p.reshape(nb, seq, D), y_s.reshape(ns, 1, D),
            ssd_state_p[None], ssd_conv_p[None], gdn_state_p[None], gdn_conv_p[None],
            ssd_state_s[None], ssd_conv_s.reshape(1, ns, CONV_W - 1, SSD_CONV),
            gdn_state_s[None], gdn_conv_s.reshape(1, ns, CONV_W - 1, GDN_CONV))
```

```python
import functools
import math

import jax
import jax.numpy as jnp
from jax import lax
from jax.experimental import pallas as pl
from jax.experimental.pallas import tpu as pltpu

f32 = jnp.float32
bf16 = jnp.bfloat16

D = 2048
CONV_W = 4
SSD_P = 64
SSD_INNER = D // 2
SSD_H = SSD_INNER // SSD_P
SSD_G = 2
SSD_N = 128
SSD_CHUNK = 128
SSD_BC = 2 * SSD_G * SSD_N
SSD_CONV = SSD_INNER + SSD_BC
GDN_DK = 128
GDN_DV = 128
GDN_V = D // 2
GDN_H = GDN_V // GDN_DV
GDN_CHUNK = 64
GDN_CONV = GDN_H * (2 * GDN_DK + GDN_DV)
PEER_H = 8
PEER_NK = 128
PEER_E = PEER_NK * PEER_NK
PEER_DK = 256
PEER_K = 16
EPS = 1e-6
NEG = -1e30

OFF_Z, OFF_XS, OFF_GATE, OFF_QKV, OFF_BC, OFF_SM = 0, 1024, 2048, 3072, 6144, 6656
P_COLS = 7168
SM_DT, SM_A, SM_B = 0, 16, 24

VMEM_LIMIT = 56 * 1024 * 1024


def _cp(sem):
    return pltpu.CompilerParams(dimension_semantics=sem, vmem_limit_bytes=VMEM_LIMIT)


def _silu(x):
    return x * jax.nn.sigmoid(x)


def _softplus(x):
    return jnp.maximum(x, 0.0) + jnp.log1p(jnp.exp(-jnp.abs(x)))


def _dot(a, b):
    return jnp.dot(a, b, preferred_element_type=f32)


def _dot_nt(a, b):
    return lax.dot_general(a, b, (((1,), (1,)), ((), ())), preferred_element_type=f32)


def _dot_hi(a, b):
    return jnp.dot(a, b, preferred_element_type=f32, precision=lax.Precision.HIGHEST)


def _iota(shape, dim):
    return lax.broadcasted_iota(jnp.int32, shape, dim)


def _mod_kernel(c_ref, w_ref, b_ref, o_ref):
    a = _silu(c_ref[...]).astype(bf16)
    o_ref[...] = _dot(a, w_ref[...].astype(bf16)) + b_ref[...]


def _modulation(c_all, w_ada, b_ada):
    m, tn = c_all.shape[0], 1024
    return pl.pallas_call(
        _mod_kernel,
        grid=(6 * D // tn,),
        in_specs=[pl.BlockSpec((m, D), lambda j: (0, 0)),
                  pl.BlockSpec((D, tn), lambda j: (0, j)),
                  pl.BlockSpec((1, tn), lambda j: (0, j))],
        out_specs=pl.BlockSpec((m, tn), lambda j: (0, j)),
        out_shape=jax.ShapeDtypeStruct((m, 6 * D), f32),
        compiler_params=_cp(("arbitrary",)),
        name="adaln_mod",
    )(c_all, w_ada, b_ada.reshape(1, 6 * D))


def _mod_specs(per_token, tm, rows_per_seq, which, grid_rank):
    if per_token:
        if grid_rank == 2:
            return pl.BlockSpec((tm, D), lambda i, j: (i, which))
        return pl.BlockSpec((tm, D), lambda i: (i, which))
    tps = rows_per_seq // tm
    if grid_rank == 2:
        return pl.BlockSpec((None, 1, D), lambda i, j: (i // tps, 0, which))
    return pl.BlockSpec((None, 1, D), lambda i: (i // tps, 0, which))


def _inproj_kernel(x_ref, sh_ref, sc_ref, nw_ref, w_ref, o_ref, hm_ref):
    @pl.when(pl.program_id(1) == 0)
    def _():
        x = x_ref[...]
        y = x * lax.rsqrt(jnp.mean(x * x, axis=-1, keepdims=True) + EPS) * nw_ref[...]
        hm_ref[...] = (y * (1.0 + sc_ref[...]) + sh_ref[...]).astype(bf16)

    o_ref[...] = _dot(hm_ref[...], w_ref[...])


def _inproj(x, mod, norm_w, w_re, per_token, rows_per_seq):
    t = x.shape[0]
    tm = min(1024, t if per_token else rows_per_seq)
    assert t % tm == 0 and (per_token or rows_per_seq % tm == 0)
    tn = 1024
    return pl.pallas_call(
        _inproj_kernel,
        grid=(t // tm, P_COLS // tn),
        in_specs=[pl.BlockSpec((tm, D), lambda i, j: (i, 0)),
                  _mod_specs(per_token, tm, rows_per_seq, 0, 2),
                  _mod_specs(per_token, tm, rows_per_seq, 1, 2),
                  pl.BlockSpec((1, D), lambda i, j: (0, 0)),
                  pl.BlockSpec((D, tn), lambda i, j: (0, j))],
        out_specs=pl.BlockSpec((tm, tn), lambda i, j: (i, j)),
        out_shape=jax.ShapeDtypeStruct((t, P_COLS), f32),
        scratch_shapes=[pltpu.VMEM((tm, D), bf16)],
        compiler_params=_cp(("parallel", "arbitrary")),
        name="inproj",
    )(x, mod, mod, norm_w.reshape(1, D), w_re)


def _conv_chunk(ext_ref, cur, w_ref, b_row, q):
    ext_ref[8:8 + q, :] = cur
    acc = w_ref[CONV_W - 1:CONV_W, :] * cur
    for k in range(CONV_W - 1):
        acc = acc + w_ref[k:k + 1, :] * ext_ref[5 + k:5 + k + q, :]
    ext_ref[0:8, :] = cur[q - 8:q, :]
    if b_row is not None:
        acc = acc + b_row
    return acc


def _group_rmsnorm(y, nw, width):
    outs = []
    for g in range(y.shape[-1] // width):
        seg = y[:, g * width:(g + 1) * width]
        ms = jnp.mean(seg * seg, axis=-1, keepdims=True)
        outs.append(seg * lax.rsqrt(ms + EPS) * nw[:, g * width:(g + 1) * width])
    return jnp.concatenate(outs, axis=-1)


def _lane_mask(shape, lo, hi):
    lane = _iota(shape, len(shape) - 1)
    return (lane >= lo) & (lane < hi)


def _ssd_chunk(z_ref, xs_ref, bc_ref, sm_ref, y_ref, extx, extb, ht, yscr, wxs_ref, wbc_ref, bxs_ref, bbc_ref,
               dtb_ref, alog_ref, de_ref, nw_ref):
    q = SSD_CHUNK
    xs = _silu(_conv_chunk(extx, xs_ref[...], wxs_ref, bxs_ref[...], q))
    bcv = _silu(_conv_chunk(extb, bc_ref[...], wbc_ref, bbc_ref[...], q))

    head_lanes = _lane_mask((q, 128), SM_DT, SM_DT + SSD_H)
    dt = jnp.where(head_lanes, _softplus(sm_ref[...] + dtb_ref[...]), 0.0)
    a_row = -jnp.exp(alog_ref[...])
    tri = (_iota((q, q), 0) >= _iota((q, q), 1))
    cs = _dot_hi(tri.astype(f32), dt * a_row)
    cs_t = cs.T
    dt_t = dt.T
    last_col = cs_t[:, q - 1:q]
    w_t = dt_t * jnp.exp(last_col - cs_t)
    ecs = jnp.exp(cs)
    lane_lo = _iota((q, 128), 1) < SSD_P

    for g in range(SSD_G):
        b_g = bcv[:, g * SSD_N:(g + 1) * SSD_N]
        c_g = bcv[:, SSD_G * SSD_N + g * SSD_N:SSD_G * SSD_N + (g + 1) * SSD_N]
        b_gt = b_g.T
        cb = _dot(c_g.astype(bf16), b_gt.astype(bf16))
        for r2 in range(SSD_H // SSD_G // 2):
            pair = g * (SSD_H // SSD_G // 2) + r2
            xs_pair = xs[:, pair * 128:(pair + 1) * 128]
            xs_pair_b = xs_pair.astype(bf16)
            h_prev = ht[pair]
            rhs = jnp.concatenate([xs_pair_b, h_prev.astype(bf16)], axis=0)
            ys, hs, els = [], [], []
            for e in range(2):
                h = 2 * pair + e
                diff = cs[:, h:h + 1] - cs_t[h:h + 1, :]
                dec = jnp.exp(jnp.where(tri, diff, NEG))
                m = cb * dec * dt_t[h:h + 1, :]
                lhs = jnp.concatenate([m, c_g * ecs[:, h:h + 1]], axis=1).astype(bf16)
                ys.append(_dot(lhs, rhs))
                bw = (b_gt * w_t[h:h + 1, :]).astype(bf16)
                hs.append(_dot(bw, xs_pair_b))
                els.append(jnp.exp(last_col[h:h + 1, :]))
            yscr[:, pair * 128:(pair + 1) * 128] = jnp.where(lane_lo, ys[0], ys[1])
            ht[pair] = h_prev * jnp.where(lane_lo, els[0], els[1]) + jnp.where(lane_lo, hs[0], hs[1])

    z = z_ref[...]
    y = (yscr[...] + de_ref[...] * xs) * _silu(z)
    y_ref[...] = _group_rmsnorm(y, nw_ref[...], SSD_INNER // SSD_G).astype(y_ref.dtype)


def _ssd_kernel(z_ref, xs_ref, bc_ref, sm_ref, wxs_ref, wbc_ref, bxs_ref, bbc_ref, dtb_ref, alog_ref, de_ref,
                nw_ref, y_ref, st_ref, extx, extb, ht, yscr):
    n_seq = z_ref.shape[0]
    c = pl.program_id(1)

    @pl.when(c == 0)
    def _():
        for s in range(n_seq):
            extx[s, 0:8, :] = jnp.zeros((8, SSD_INNER), f32)
            extb[s, 0:8, :] = jnp.zeros((8, SSD_BC), f32)
        ht[...] = jnp.zeros(ht.shape, f32)

    for s in range(n_seq):
        _ssd_chunk(z_ref.at[s], xs_ref.at[s], bc_ref.at[s], sm_ref.at[s], y_ref.at[s], extx.at[s], extb.at[s],
                   ht.at[s], yscr.at[s], wxs_ref, wbc_ref, bxs_ref, bbc_ref, dtb_ref, alog_ref, de_ref, nw_ref)

    @pl.when(c == pl.num_programs(1) - 1)
    def _():
        for s in range(n_seq):
            for pair in range(SSD_H // 2):
                t = ht[s, pair].T
                st_ref[s, 2 * pair] = t[0:SSD_P]
                st_ref[s, 2 * pair + 1] = t[SSD_P:2 * SSD_P]


def _ssd_prompt(proj, nb, seq, w):
    nc = seq // SSD_CHUNK
    q = SSD_CHUNK
    n_seq = 2 if nb % 2 == 0 else 1
    proj3 = proj.reshape(nb, seq, P_COLS)
    const = lambda shape: pl.BlockSpec(shape, lambda n, c: (0, 0))
    y, st = pl.pallas_call(
        _ssd_kernel,
        grid=(nb // n_seq, nc),
        in_specs=[pl.BlockSpec((n_seq, q, 1024), lambda n, c: (n, c, OFF_Z // 1024)),
                  pl.BlockSpec((n_seq, q, 1024), lambda n, c: (n, c, OFF_XS // 1024)),
                  pl.BlockSpec((n_seq, q, SSD_BC), lambda n, c: (n, c, OFF_BC // SSD_BC)),
                  pl.BlockSpec((n_seq, q, 128), lambda n, c: (n, c, OFF_SM // 128)),
                  const((CONV_W, SSD_INNER)), const((CONV_W, SSD_BC)), const((1, SSD_INNER)), const((1, SSD_BC)),
                  const((1, 128)), const((1, 128)), const((1, SSD_INNER)), const((1, SSD_INNER))],
        out_specs=[pl.BlockSpec((n_seq, q, SSD_INNER), lambda n, c: (n, c, 0)),
                   pl.BlockSpec((n_seq, SSD_H, SSD_P, SSD_N), lambda n, c: (n, 0, 0, 0))],
        out_shape=[jax.ShapeDtypeStruct((nb, seq, SSD_INNER), bf16),
                   jax.ShapeDtypeStruct((nb, SSD_H, SSD_P, SSD_N), f32)],
        scratch_shapes=[pltpu.VMEM((n_seq, q + 8, SSD_INNER), f32), pltpu.VMEM((n_seq, q + 8, SSD_BC), f32),
                        pltpu.VMEM((n_seq, SSD_H // 2, SSD_N, 2 * SSD_P), f32),
                        pltpu.VMEM((n_seq, q, SSD_INNER), f32)],
        compiler_params=_cp(("parallel", "arbitrary")),
        name="ssd_scan",
    )(proj3, proj3, proj3, proj3, w["ssd_wxs"], w["ssd_wbc"], w["ssd_bxs"], w["ssd_bbc"], w["ssd_dtb"],
      w["ssd_alog"], w["ssd_de"], w["ssd_nw"])
    return y.reshape(nb * seq, SSD_INNER), st


def _gdn_kernel(qkv_ref, gate_ref, sm_ref, cw_ref, dtb_ref, alog_ref, nw_ref, o_ref, st_ref, ext, s_scr):
    q = GDN_CHUNK
    n_seq = qkv_ref.shape[0]
    c = pl.program_id(1)

    @pl.when(c == 0)
    def _():
        for s in range(n_seq):
            ext[s, 0:8, :] = jnp.zeros((8, GDN_CONV), f32)
        s_scr[...] = jnp.zeros(s_scr.shape, f32)

    incl = _iota((q, q), 0) >= _iota((q, q), 1)
    strict = _iota((q, q), 0) > _iota((q, q), 1)
    rb, cb_ = _iota((q, q), 0), _iota((q, q), 1)
    same16 = (rb // 16) == (cb_ // 16)
    same32 = (rb // 32) == (cb_ // 32)
    diag_mask = strict & same16
    off32_mask = strict & same32 & jnp.logical_not(same16)
    off64_mask = strict & jnp.logical_not(same32)
    a_lanes = _lane_mask((q, 128), SM_A, SM_A + GDN_H)
    nw = nw_ref[...]

    heads = range(n_seq * GDN_H)
    qn, kn, kb, dec, ecol, bcol, gcol, vh = [], [], [], [], [], [], [], []
    for s in range(n_seq):
        act = _silu(_conv_chunk(ext.at[s], qkv_ref[s], cw_ref, None, q))
        sm = sm_ref[s]
        g = jnp.where(a_lanes, -jnp.exp(alog_ref[...]) * _softplus(sm + dtb_ref[...]), 0.0)
        beta = jax.nn.sigmoid(sm)
        gc = _dot_hi(incl.astype(f32), g)
        gc_t = gc.T
        egc = jnp.exp(gc)
        for h in range(GDN_H):
            qh = act[:, h * GDN_DK:(h + 1) * GDN_DK]
            kh = act[:, GDN_H * GDN_DK + h * GDN_DK:GDN_H * GDN_DK + (h + 1) * GDN_DK]
            vh.append(act[:, 2 * GDN_H * GDN_DK + h * GDN_DV:2 * GDN_H * GDN_DK + (h + 1) * GDN_DV])
            qn.append(qh * lax.rsqrt(jnp.sum(qh * qh, axis=-1, keepdims=True) + EPS) * (GDN_DK ** -0.5))
            kn.append(kh * lax.rsqrt(jnp.sum(kh * kh, axis=-1, keepdims=True) + EPS))
            gcol.append(gc[:, SM_A + h:SM_A + h + 1])
            grow = gc_t[SM_A + h:SM_A + h + 1, :]
            ecol.append(egc[:, SM_A + h:SM_A + h + 1])
            bcol.append(beta[:, SM_B + h:SM_B + h + 1])
            dec.append(jnp.exp(jnp.where(incl, gcol[-1] - grow, NEG)))
            kb.append(kn[-1] * bcol[-1])
    kn_b = [kn[h].astype(bf16) for h in heads]
    lmat = [_dot_nt(kb[h].astype(bf16), kn_b[h]) * dec[h] for h in heads]
    qk = [_dot_nt(qn[h].astype(bf16), kn_b[h]) * dec[h] for h in heads]

    dotb = lambda a, b: _dot(a.astype(bf16), b.astype(bf16))
    n_pow = [jnp.where(diag_mask, -lmat[h], 0.0) for h in heads]
    t_m = list(n_pow)
    for _ in range(3):
        n_pow = [dotb(n_pow[h], n_pow[h]) for h in heads]
        t_m = [t_m[h] + n_pow[h] + dotb(t_m[h], n_pow[h]) for h in heads]
    for mask in (off32_mask, off64_mask):
        off = [jnp.where(mask, lmat[h], 0.0) for h in heads]
        y = [off[h] + dotb(off[h], t_m[h]) for h in heads]
        t_m = [t_m[h] - (y[h] + dotb(t_m[h], y[h])) for h in heads]

    rhs = [jnp.concatenate([vh[h] * bcol[h], kb[h] * ecol[h]], axis=1) for h in heads]
    sol = [rhs[h] + dotb(t_m[h], rhs[h]) for h in heads]
    s_prev = [s_scr[h] for h in heads]
    s_b = [s_prev[h].astype(bf16) for h in heads]
    v_new = [sol[h][:, :GDN_DV] - _dot(sol[h][:, GDN_DV:].astype(bf16), s_b[h]) for h in heads]
    v_new_b = [v_new[h].astype(bf16) for h in heads]
    o = [_dot((qn[h] * ecol[h]).astype(bf16), s_b[h]) + _dot(qk[h].astype(bf16), v_new_b[h]) for h in heads]
    for h in heads:
        glast = gcol[h][q - 1:q, :]
        kdec = kn[h] * jnp.exp(glast - gcol[h])
        s_scr[h] = s_prev[h] * jnp.exp(glast) + _dot(kdec.T.astype(bf16), v_new_b[h])
    for s in range(n_seq):
        gate = gate_ref[s]
        for h in range(GDN_H):
            oh = o[s * GDN_H + h]
            on = oh * lax.rsqrt(jnp.mean(oh * oh, axis=-1, keepdims=True) + EPS) * nw
            o_ref[s, :, h * GDN_DV:(h + 1) * GDN_DV] = (
                on * _silu(gate[:, h * GDN_DV:(h + 1) * GDN_DV])).astype(o_ref.dtype)

    @pl.when(c == pl.num_programs(1) - 1)
    def _():
        for s in range(n_seq):
            st_ref[s] = s_scr[s * GDN_H:(s + 1) * GDN_H]


def _gdn_prompt(proj, nb, seq, w):
    nc = seq // GDN_CHUNK
    q = GDN_CHUNK
    n_seq = 4 if nb % 4 == 0 else (2 if nb % 2 == 0 else 1)
    proj3 = proj.reshape(nb, seq, P_COLS)
    const = lambda shape: pl.BlockSpec(shape, lambda n, c: (0, 0))
    o, st = pl.pallas_call(
        _gdn_kernel,
        grid=(nb // n_seq, nc),
        in_specs=[pl.BlockSpec((n_seq, q, GDN_CONV), lambda n, c: (n, c, OFF_QKV // GDN_CONV)),
                  pl.BlockSpec((n_seq, q, GDN_V), lambda n, c: (n, c, OFF_GATE // GDN_V)),
                  pl.BlockSpec((n_seq, q, 128), lambda n, c: (n, c, OFF_SM // 128)),
                  const((CONV_W, GDN_CONV)), const((1, 128)), const((1, 128)), const((1, GDN_DV))],
        out_specs=[pl.BlockSpec((n_seq, q, GDN_V), lambda n, c: (n, c, 0)),
                   pl.BlockSpec((n_seq, GDN_H, GDN_DK, GDN_DV), lambda n, c: (n, 0, 0, 0))],
        out_shape=[jax.ShapeDtypeStruct((nb, seq, GDN_V), bf16),
                   jax.ShapeDtypeStruct((nb, GDN_H, GDN_DK, GDN_DV), f32)],
        scratch_shapes=[pltpu.VMEM((n_seq, q + 8, GDN_CONV), f32),
                        pltpu.VMEM((n_seq * GDN_H, GDN_DK, GDN_DV), f32)],
        compiler_params=_cp(("parallel", "arbitrary")),
        name="gdn_scan",
    )(proj3, proj3, proj3, w["gdn_cw"], w["gdn_dtb"], w["gdn_alog"], w["gdn_nw"])
    return o.reshape(nb * seq, GDN_V), st


def _conv_step(buf, xraw, w_ref, width):
    acc = w_ref[CONV_W - 1:CONV_W, :] * xraw
    for k in range(CONV_W - 1):
        acc = acc + w_ref[k:k + 1, :] * buf[:, k * width:(k + 1) * width]
    return acc, jnp.concatenate([buf[:, width:], xraw], axis=1)


def _ssd_step_kernel(z_ref, xs_ref, bc_ref, sm_ref, buf_ref, st_ref, cw_ref, cb_ref, dtb_ref, alog_ref, de_ref,
                     nw_ref, ex_ref, y_ref, nbuf_ref, nst_ref, yscr):
    bs = z_ref.shape[0]
    xraw = jnp.concatenate([xs_ref[...], bc_ref[...]], axis=1)
    conv, nbuf = _conv_step(buf_ref[...], xraw, cw_ref, SSD_CONV)
    nbuf_ref[...] = nbuf
    act = _silu(conv + cb_ref[...])
    xs = act[:, :SSD_INNER]
    head_lanes = _lane_mask((bs, 128), SM_DT, SM_DT + SSD_H)
    dt = jnp.where(head_lanes, _softplus(sm_ref[...] + dtb_ref[...]), 0.0)
    da = jnp.exp(dt * (-jnp.exp(alog_ref[...])))
    dt_e = _dot_hi(dt, ex_ref[...])
    da_e = _dot_hi(da, ex_ref[...])
    cols = jnp.concatenate([xs * dt_e, da_e], axis=0).T
    gw = SSD_INNER // SSD_G
    hg = SSD_H // SSD_G
    for b in range(bs):
        xcol = cols[:, b:b + 1]
        dcol = cols[:, bs + b:bs + b + 1]
        for g in range(SSD_G):
            b_row = act[b:b + 1, SSD_INNER + g * SSD_N:SSD_INNER + (g + 1) * SSD_N]
            c_row = act[b:b + 1, SSD_INNER + (SSD_G + g) * SSD_N:SSD_INNER + (SSD_G + g + 1) * SSD_N]
            st = st_ref[b, g * hg:(g + 1) * hg].reshape(gw, SSD_N)
            new = st * dcol[g * gw:(g + 1) * gw] + xcol[g * gw:(g + 1) * gw] * b_row
            nst_ref[b, g * hg:(g + 1) * hg] = new.reshape(hg, SSD_P, SSD_N)
            c8 = jnp.broadcast_to(c_row, (8, SSD_N))
            yscr[b:b + 1, g * gw:(g + 1) * gw] = _dot_nt(c8, new)[0:1]
    y = (yscr[...] + de_ref[...] * xs) * _silu(z_ref[...])
    y_ref[...] = _group_rmsnorm(y, nw_ref[...], gw).astype(y_ref.dtype)


def _ssd_step(proj, buf, state, w, bs=8):
    n = proj.shape[0]
    const = lambda shape: pl.BlockSpec(shape, lambda i: (0,) * len(shape))
    return pl.pallas_call(
        _ssd_step_kernel,
        grid=(n // bs,),
        in_specs=[pl.BlockSpec((bs, 1024), lambda i: (i, OFF_Z // 1024)),
                  pl.BlockSpec((bs, 1024), lambda i: (i, OFF_XS // 1024)),
                  pl.BlockSpec((bs, SSD_BC), lambda i: (i, OFF_BC // SSD_BC)),
                  pl.BlockSpec((bs, 128), lambda i: (i, OFF_SM // 128)),
                  pl.BlockSpec((bs, 3 * SSD_CONV), lambda i: (i, 0)),
                  pl.BlockSpec((bs, SSD_H, SSD_P, SSD_N), lambda i: (i, 0, 0, 0)),
                  const((CONV_W, SSD_CONV)), const((1, SSD_CONV)), const((1, 128)), const((1, 128)),
                  const((1, SSD_INNER)), const((1, SSD_INNER)), const((128, SSD_INNER))],
        out_specs=[pl.BlockSpec((bs, SSD_INNER), lambda i: (i, 0)),
                   pl.BlockSpec((bs, 3 * SSD_CONV), lambda i: (i, 0)),
                   pl.BlockSpec((bs, SSD_H, SSD_P, SSD_N), lambda i: (i, 0, 0, 0))],
        out_shape=[jax.ShapeDtypeStruct((n, SSD_INNER), bf16),
                   jax.ShapeDtypeStruct((n, 3 * SSD_CONV), f32),
                   jax.ShapeDtypeStruct((n, SSD_H, SSD_P, SSD_N), f32)],
        scratch_shapes=[pltpu.VMEM((bs, SSD_INNER), f32)],
        compiler_params=_cp(("parallel",)),
        name="ssd_step",
    )(proj, proj, proj, proj, buf, state, w["ssd_cw"], w["ssd_cb"], w["ssd_dtb"], w["ssd_alog"], w["ssd_de"],
      w["ssd_nw"], w["ssd_expand"])


def _gdn_step_kernel(qkv_ref, gate_ref, sm_ref, buf_ref, st_ref, cw_ref, dtb_ref, alog_ref, nw_ref,
                     o_ref, nbuf_ref, nst_ref, oscr):
    bs = qkv_ref.shape[0]
    conv, nbuf = _conv_step(buf_ref[...], qkv_ref[...], cw_ref, GDN_CONV)
    nbuf_ref[...] = nbuf
    act = _silu(conv)
    sm = sm_ref[...]
    eg = jnp.exp(-jnp.exp(alog_ref[...]) * _softplus(sm + dtb_ref[...]))
    beta = jax.nn.sigmoid(sm)
    qs, ks = [], []
    for h in range(GDN_H):
        qh = act[:, h * GDN_DK:(h + 1) * GDN_DK]
        kh = act[:, GDN_H * GDN_DK + h * GDN_DK:GDN_H * GDN_DK + (h + 1) * GDN_DK]
        qs.append(qh * lax.rsqrt(jnp.sum(qh * qh, axis=-1, keepdims=True) + EPS) * (GDN_DK ** -0.5))
        ks.append(kh * lax.rsqrt(jnp.sum(kh * kh, axis=-1, keepdims=True) + EPS))
    qk_t = jnp.concatenate(qs + ks, axis=1).T
    for b in range(bs):
        for h in range(GDN_H):
            qcol = qk_t[h * GDN_DK:(h + 1) * GDN_DK, b:b + 1]
            kcol = qk_t[(GDN_H + h) * GDN_DK:(GDN_H + h + 1) * GDN_DK, b:b + 1]
            vrow = act[b:b + 1, 2 * GDN_H * GDN_DK + h * GDN_DV:2 * GDN_H * GDN_DK + (h + 1) * GDN_DV]
            egs = eg[b:b + 1, SM_A + h:SM_A + h + 1]
            bet = beta[b:b + 1, SM_B + h:SM_B + h + 1]
            s = st_ref[b, h]
            v_new = bet * (vrow - egs * jnp.sum(kcol * s, axis=0, keepdims=True))
            s_new = s * egs + kcol * v_new
            nst_ref[b, h] = s_new
            oscr[b:b + 1, h * GDN_DV:(h + 1) * GDN_DV] = jnp.sum(qcol * s_new, axis=0, keepdims=True)
    o = oscr[...]
    nw = nw_ref[...]
    gate = gate_ref[...]
    for h in range(GDN_H):
        oh = o[:, h * GDN_DV:(h + 1) * GDN_DV]
        on = oh * lax.rsqrt(jnp.mean(oh * oh, axis=-1, keepdims=True) + EPS) * nw
        o_ref[:, h * GDN_DV:(h + 1) * GDN_DV] = (on * _silu(gate[:, h * GDN_DV:(h + 1) * GDN_DV])).astype(o_ref.dtype)


def _gdn_step(proj, buf, state, w, bs=8):
    n = proj.shape[0]
    const = lambda shape: pl.BlockSpec(shape, lambda i: (0,) * len(shape))
    return pl.pallas_call(
        _gdn_step_kernel,
        grid=(n // bs,),
        in_specs=[pl.BlockSpec((bs, GDN_CONV), lambda i: (i, OFF_QKV // GDN_CONV)),
                  pl.BlockSpec((bs, GDN_V), lambda i: (i, OFF_GATE // GDN_V)),
                  pl.BlockSpec((bs, 128), lambda i: (i, OFF_SM // 128)),
                  pl.BlockSpec((bs, 3 * GDN_CONV), lambda i: (i, 0)),
                  pl.BlockSpec((bs, GDN_H, GDN_DK, GDN_DV), lambda i: (i, 0, 0, 0)),
                  const((CONV_W, GDN_CONV)), const((1, 128)), const((1, 128)), const((1, GDN_DV))],
        out_specs=[pl.BlockSpec((bs, GDN_V), lambda i: (i, 0)),
                   pl.BlockSpec((bs, 3 * GDN_CONV), lambda i: (i, 0)),
                   pl.BlockSpec((bs, GDN_H, GDN_DK, GDN_DV), lambda i: (i, 0, 0, 0))],
        out_shape=[jax.ShapeDtypeStruct((n, GDN_V), bf16),
                   jax.ShapeDtypeStruct((n, 3 * GDN_CONV), f32),
                   jax.ShapeDtypeStruct((n, GDN_H, GDN_DK, GDN_DV), f32)],
        scratch_shapes=[pltpu.VMEM((bs, GDN_V), f32)],
        compiler_params=_cp(("parallel",)),
        name="gdn_step",
    )(proj, proj, proj, buf, state, w["gdn_cw"], w["gdn_dtb"], w["gdn_alog"], w["gdn_nw"])


def _outproj_kernel(ys_ref, og_ref, x_ref, g1_ref, sh_ref, sc_ref, nw_ref, w_ref, x1_ref, hf_ref):
    m = _dot(ys_ref[...], w_ref[0:SSD_INNER, :]) + _dot(og_ref[...], w_ref[SSD_INNER:, :])
    x1 = x_ref[...] + g1_ref[...] * m
    x1_ref[...] = x1
    y = x1 * lax.rsqrt(jnp.mean(x1 * x1, axis=-1, keepdims=True) + EPS) * nw_ref[...]
    hf_ref[...] = (y * (1.0 + sc_ref[...]) + sh_ref[...]).astype(hf_ref.dtype)


def _outproj(ys, og, x, mod, norm_w, w_out_b, per_token, rows_per_seq):
    t = x.shape[0]
    tm = min(512, t)
    return pl.pallas_call(
        _outproj_kernel,
        grid=(t // tm,),
        in_specs=[pl.BlockSpec((tm, SSD_INNER), lambda i: (i, 0)),
                  pl.BlockSpec((tm, GDN_V), lambda i: (i, 0)),
                  pl.BlockSpec((tm, D), lambda i: (i, 0)),
                  _mod_specs(per_token, tm, rows_per_seq, 2, 1),
                  _mod_specs(per_token, tm, rows_per_seq, 3, 1),
                  _mod_specs(per_token, tm, rows_per_seq, 4, 1),
                  pl.BlockSpec((1, D), lambda i: (0, 0)),
                  pl.BlockSpec((D, D), lambda i: (0, 0))],
        out_specs=[pl.BlockSpec((tm, D), lambda i: (i, 0)), pl.BlockSpec((tm, D), lambda i: (i, 0))],
        out_shape=[jax.ShapeDtypeStruct((t, D), f32), jax.ShapeDtypeStruct((t, D), bf16)],
        compiler_params=_cp(("parallel",)),
        name="outproj",
    )(ys, og, x, mod, mod, mod, norm_w.reshape(1, D), w_out_b)


def _pick_token_tile(i, n_main, main_ref, tail_ref, dst_ref):
    @pl.when(i < n_main)
    def _():
        dst_ref[...] = main_ref[...]

    @pl.when(i >= n_main)
    def _():
        dst_ref[...] = tail_ref[...]


def _peer_scores_kernel(hfp_ref, hfs_ref, wq_ref, sk_ref, o_ref, hf_scr):
    _pick_token_tile(pl.program_id(0), pl.num_programs(0) - 1, hfp_ref, hfs_ref, hf_scr)
    q_t = _dot_nt(wq_ref[...], hf_scr[...])
    half = PEER_DK // 2
    for hs in range(2 * PEER_H):
        o_ref[hs] = _dot(sk_ref[hs], q_t[hs * half:(hs + 1) * half].astype(bf16))


def _peer_scores(hf_p, hf_s, wq_b, sk, tm):
    n_main = hf_p.shape[0] // tm
    t = (n_main + 1) * tm
    return pl.pallas_call(
        _peer_scores_kernel,
        grid=(n_main + 1,),
        in_specs=[pl.BlockSpec((tm, D), lambda i: (jnp.minimum(i, n_main - 1), 0)),
                  pl.BlockSpec((tm, D), lambda i: (0, 0)),
                  pl.BlockSpec((PEER_H * PEER_DK, D), lambda i: (0, 0)),
                  pl.BlockSpec((2 * PEER_H, PEER_NK, PEER_DK // 2), lambda i: (0, 0, 0))],
        out_specs=pl.BlockSpec((2 * PEER_H, PEER_NK, tm), lambda i: (0, 0, i)),
        out_shape=jax.ShapeDtypeStruct((2 * PEER_H, PEER_NK, t), f32),
        scratch_shapes=[pltpu.VMEM((tm, D), bf16)],
        compiler_params=_cp(("arbitrary",)),
        name="peer_scores",
    )(hf_p, hf_s, wq_b, sk)


_STAIR = [PEER_K // (k1 + 1) for k1 in range(PEER_K)]
_STAIR_ROWS = -(-sum(_STAIR) // 8) * 8


def _peer_select_kernel(s_ref, rank_ref, cnt_ref, f0_ref, e1_ref, v1_scr, cand):
    tm = s_ref.shape[-1]
    for h in range(PEER_H):
        s0 = s_ref[2 * h]
        s1 = s_ref[2 * h + 1]
        tops = []
        rank1 = jnp.full((PEER_NK, tm), float(PEER_K), f32)
        for side, cur in enumerate((s0, s1)):
            vals = []
            for k in range(PEER_K):
                m = jnp.max(cur, axis=0, keepdims=True)
                vals.append(m)
                hit = cur == m
                if side == 1:
                    rank1 = jnp.where(hit, float(k), rank1)
                cur = jnp.where(hit, -jnp.inf, cur)
            tops.append(vals)
        for k in range(PEER_K):
            v1_scr[k:k + 1, :] = tops[1][k]
        cand[_STAIR_ROWS - 8:_STAIR_ROWS, :] = jnp.full((8, tm), -jnp.inf, f32)
        off = 0
        for k1 in range(PEER_K):
            cand[off:off + _STAIR[k1], :] = tops[0][k1] + v1_scr[0:_STAIR[k1], :]
            off += _STAIR[k1]
        cv = cand[...]
        cur = cv
        for _ in range(PEER_K):
            tau = jnp.max(cur, axis=0, keepdims=True)
            cur = jnp.where(cur == tau, -jnp.inf, cur)
        m0, m1 = tops[0][0], tops[1][0]
        zsum = jnp.sum(jnp.where(cv >= tau, jnp.exp(cv - (m0 + m1)), 0.0), axis=0, keepdims=True)
        cnt = jnp.zeros((PEER_NK, tm), f32)
        for k1 in range(PEER_K):
            v1 = v1_scr[0:_STAIR[k1], :]
            n_k = jnp.sum(jnp.where(tops[0][k1] + v1 >= tau, 1.0, 0.0), axis=0, keepdims=True)
            cnt = jnp.where(s0 == tops[0][k1], n_k, cnt)
        rank_ref[h] = rank1.astype(bf16)
        cnt_ref[h] = cnt
        f0_ref[h] = jnp.exp(s0 - m0) / zsum
        e1_ref[h] = jnp.exp(s1 - m1).astype(bf16)


def _peer_select(scores_t, tm):
    t = scores_t.shape[-1]
    spec = pl.BlockSpec((PEER_H, PEER_NK, tm), lambda i: (0, 0, i))
    shape = lambda dt: jax.ShapeDtypeStruct((PEER_H, PEER_NK, t), dt)
    return pl.pallas_call(
        _peer_select_kernel,
        grid=(t // tm,),
        in_specs=[pl.BlockSpec((2 * PEER_H, PEER_NK, tm), lambda i: (0, 0, i))],
        out_specs=[spec, spec, spec, spec],
        out_shape=[shape(bf16), shape(f32), shape(f32), shape(bf16)],
        scratch_shapes=[pltpu.VMEM((PEER_K, tm), f32), pltpu.VMEM((_STAIR_ROWS, tm), f32)],
        compiler_params=_cp(("parallel",)),
        name="peer_select",
    )(scores_t)


def _gelu(x):
    return 0.5 * x * (1.0 + lax.erf(x * (1.0 / math.sqrt(2.0))))


def _peer_dense_kernel(hfp_ref, hfs_ref, u_ref, v_ref, rank_ref, cnt_ref, f0_ref, e1_ref, o_ref, cd, hf_scr):
    tm = hf_scr.shape[0]
    te = u_ref.shape[0]
    sub = PEER_SUB_ROWS

    @pl.when(pl.program_id(1) == 0)
    def _():
        o_ref[...] = jnp.zeros(o_ref.shape, f32)
        _pick_token_tile(pl.program_id(0), pl.num_programs(0) - 1, hfp_ref, hfs_ref, hf_scr)

    a_t = _dot_nt(u_ref[...], hf_scr[...])
    for i1 in range(te // PEER_NK):
        for c in range(0, tm, 128):
            lanes = slice(c, c + 128)
            cnt_b = [jnp.broadcast_to(cnt_ref[h, i1:i1 + 1, lanes], (sub, 128)).astype(bf16) for h in range(PEER_H)]
            f0_b = [jnp.broadcast_to(f0_ref[h, i1:i1 + 1, lanes], (sub, 128)).astype(bf16) for h in range(PEER_H)]
            for r in range(0, PEER_NK, sub):
                wb = None
                for h in range(PEER_H):
                    keep = rank_ref[h, r:r + sub, lanes] < cnt_b[h]
                    term = jnp.where(keep, e1_ref[h, r:r + sub, lanes], jnp.zeros((), bf16)) * f0_b[h]
                    wb = term if wb is None else wb + term
                rows = slice(i1 * PEER_NK + r, i1 * PEER_NK + r + sub)
                cd[rows, lanes] = wb * _gelu(a_t[rows, lanes]).astype(bf16)
    o_ref[...] += lax.dot_general(cd[...], v_ref[...], (((0,), (0,)), ((), ())), preferred_element_type=f32)


def _peer_dense(hf_p, hf_s, u_b, v_b, rank1, cnt, f0, e1, tm, te):
    n_main = hf_p.shape[0] // tm
    t = (n_main + 1) * tm
    ni1 = te // PEER_NK
    assert ni1 == 8, "one 8-row block of side-0 rows per expert tile"
    rows8 = pl.BlockSpec((PEER_H, ni1, tm), lambda i, j: (0, j, i))
    full = pl.BlockSpec((PEER_H, PEER_NK, tm), lambda i, j: (0, 0, i))
    return pl.pallas_call(
        _peer_dense_kernel,
        grid=(n_main + 1, PEER_E // te),
        in_specs=[pl.BlockSpec((tm, D), lambda i, j: (jnp.minimum(i, n_main - 1), 0)),
                  pl.BlockSpec((tm, D), lambda i, j: (0, 0)),
                  pl.BlockSpec((te, D), lambda i, j: (j, 0)),
                  pl.BlockSpec((te, D), lambda i, j: (j, 0)),
                  full, rows8, rows8, full],
        out_specs=pl.BlockSpec((tm, D), lambda i, j: (i, 0)),
        out_shape=jax.ShapeDtypeStruct((t, D), f32),
        scratch_shapes=[pltpu.VMEM((te, tm), bf16), pltpu.VMEM((tm, D), bf16)],
        compiler_params=_cp(("arbitrary", "arbitrary")),
        name="peer_dense",
    )(hf_p, hf_s, u_b, v_b, rank1, cnt, f0, e1)


def _final_kernel(x1_ref, p_ref, g2_ref, nw_ref, o_ref):
    x2 = x1_ref[...] + g2_ref[...] * p_ref[...]
    o_ref[...] = x2 * lax.rsqrt(jnp.mean(x2 * x2, axis=-1, keepdims=True) + EPS) * nw_ref[...]


def _final(x1, peer_out, mod, norm_w, per_token, rows_per_seq, row_off):
    t = x1.shape[0]
    tm = min(512, t)
    off = row_off // tm
    return pl.pallas_call(
        _final_kernel,
        grid=(t // tm,),
        in_specs=[pl.BlockSpec((tm, D), lambda i: (i, 0)),
                  pl.BlockSpec((tm, D), lambda i: (i + off, 0)),
                  _mod_specs(per_token, tm, rows_per_seq, 5, 1),
                  pl.BlockSpec((1, D), lambda i: (0, 0))],
        out_specs=pl.BlockSpec((tm, D), lambda i: (i, 0)),
        out_shape=jax.ShapeDtypeStruct((t, D), f32),
        compiler_params=_cp(("parallel",)),
        name="final_norm",
    )(x1, peer_out, mod, norm_w.reshape(1, D))


def _pad_row(v, lo):
    return jnp.zeros((1, 128), f32).at[0, lo:lo + v.shape[0]].set(v.astype(f32))


def _prep_weights(w_in, ssd_conv_w, ssd_conv_b, ssd_dt_bias, ssd_A_log, ssd_D, ssd_norm_w, gdn_conv_w,
                  gdn_dt_bias, gdn_A_log, gdn_norm_w):
    o_z, o_xbc = 0, SSD_INNER
    o_dt = o_xbc + SSD_CONV
    o_qkv = o_dt + SSD_H
    o_a = o_qkv + GDN_CONV
    o_b = o_a + GDN_H
    o_gate = o_b + GDN_H
    cols = [w_in[:, o_z:o_z + SSD_INNER], w_in[:, o_xbc:o_xbc + SSD_INNER], w_in[:, o_gate:o_gate + GDN_V],
            w_in[:, o_qkv:o_qkv + GDN_CONV], w_in[:, o_xbc + SSD_INNER:o_xbc + SSD_CONV],
            w_in[:, o_dt:o_dt + SSD_H], w_in[:, o_a:o_a + GDN_H], w_in[:, o_b:o_b + GDN_H]]
    used = sum(c.shape[1] for c in cols)
    w_re = jnp.concatenate([c.astype(bf16) for c in cols] + [jnp.zeros((D, P_COLS - used), bf16)], axis=1)
    expand = (jnp.arange(128)[:, None] == (jnp.arange(SSD_INNER)[None, :] // SSD_P)).astype(f32)
    return dict(
        w_in_re=w_re,
        ssd_wxs=ssd_conv_w[:, :SSD_INNER], ssd_wbc=ssd_conv_w[:, SSD_INNER:],
        ssd_bxs=ssd_conv_b[None, :SSD_INNER], ssd_bbc=ssd_conv_b[None, SSD_INNER:],
        ssd_cw=ssd_conv_w, ssd_cb=ssd_conv_b[None, :],
        ssd_dtb=_pad_row(ssd_dt_bias, SM_DT), ssd_alog=_pad_row(ssd_A_log, SM_DT),
        ssd_de=jnp.repeat(ssd_D.astype(f32), SSD_P)[None, :], ssd_nw=ssd_norm_w[None, :].astype(f32),
        ssd_expand=expand,
        gdn_cw=gdn_conv_w, gdn_dtb=_pad_row(gdn_dt_bias, SM_A), gdn_alog=_pad_row(gdn_A_log, SM_A),
        gdn_nw=gdn_norm_w[None, :].astype(f32),
    )


PEER_TM = 512
PEER_TE = 1024
PEER_SUB_ROWS = 16


def kernel(x_prompt, x_sample, c_prompt, c_sample, state_ssd, state_ssd_conv, state_gdn, state_gdn_conv, w_ada, b_ada, norm1_w, norm2_w, w_in, ssd_conv_w, ssd_conv_b, ssd_dt_bias, ssd_A_log, ssd_D, ssd_norm_w, gdn_conv_w, gdn_dt_bias, gdn_A_log, gdn_norm_w, w_out, peer_w_q, peer_sub_keys, peer_u, peer_v, final_norm_w):
    assert w_ada.shape[0] == 1, "single layer"
    nb, seq, _ = x_prompt.shape
    ns = x_sample.shape[0]
    assert x_sample.shape[1] == 1 and seq % SSD_CHUNK == 0 and seq % 512 == 0 and ns % 8 == 0
    w = _prep_weights(w_in[0], ssd_conv_w[0], ssd_conv_b[0], ssd_dt_bias[0], ssd_A_log[0], ssd_D[0], ssd_norm_w[0],
                      gdn_conv_w[0], gdn_dt_bias[0], gdn_A_log[0], gdn_norm_w[0])
    w_out_b = w_out[0].astype(bf16)
    wq_b = peer_w_q[0].T.astype(bf16)
    sk = peer_sub_keys[0].reshape(2 * PEER_H, PEER_NK, PEER_DK // 2).astype(bf16)
    u_b = peer_u[0].astype(bf16)
    v_b = peer_v[0].astype(bf16)

    nbp = -(-nb // 8) * 8
    c_all = jnp.concatenate([jnp.pad(c_prompt, ((0, nbp - nb), (0, 0))), c_sample], axis=0)
    mod = _modulation(c_all, w_ada[0], b_ada[0])
    mod_p = mod[:nb].reshape(nb, 1, 6 * D)
    mod_s = mod[nbp:]

    xp = x_prompt.reshape(nb * seq, D)
    xs = x_sample.reshape(ns, D)

    proj_p = _inproj(xp, mod_p, norm1_w[0], w["w_in_re"], False, seq)
    y_ssd_p, ssd_state_p = _ssd_prompt(proj_p, nb, seq, w)
    o_gdn_p, gdn_state_p = _gdn_prompt(proj_p, nb, seq, w)
    x1_p, hf_p = _outproj(y_ssd_p, o_gdn_p, xp, mod_p, norm2_w[0], w_out_b, False, seq)
    tail = proj_p.reshape(nb, seq, P_COLS)[:, seq - (CONV_W - 1):, :]
    ssd_conv_p = jnp.concatenate([tail[..., OFF_XS:OFF_XS + SSD_INNER], tail[..., OFF_BC:OFF_BC + SSD_BC]], axis=-1)
    gdn_conv_p = tail[..., OFF_QKV:OFF_QKV + GDN_CONV]

    proj_s = _inproj(xs, mod_s, norm1_w[0], w["w_in_re"], True, 1)
    y_ssd_s, ssd_conv_s, ssd_state_s = _ssd_step(proj_s, state_ssd_conv[0].reshape(ns, 3 * SSD_CONV), state_ssd[0], w)
    o_gdn_s, gdn_conv_s, gdn_state_s = _gdn_step(proj_s, state_gdn_conv[0].reshape(ns, 3 * GDN_CONV), state_gdn[0], w)
    x1_s, hf_s = _outproj(y_ssd_s, o_gdn_s, xs, mod_s, norm2_w[0], w_out_b, True, 1)

    tm = PEER_TM
    assert (nb * seq) % tm == 0 and ns <= tm
    hf_s = jnp.pad(hf_s, ((0, tm - ns), (0, 0)))
    scores_t = _peer_scores(hf_p, hf_s, wq_b, sk, tm)
    rank1, cnt, f0, e1 = _peer_select(scores_t, 256)
    peer_out = _peer_dense(hf_p, hf_s, u_b, v_b, rank1, cnt, f0, e1, tm, PEER_TE)

    y_p = _final(x1_p, peer_out, mod_p, final_norm_w, False, seq, 0)
    y_s = _final(x1_s, peer_out, mod_s, final_norm_w, True, 1, nb * seq)

    return (y_p.reshape(nb, seq, D), y_s.reshape(ns, 1, D),
            ssd_state_p[None], ssd_conv_p[None], gdn_state_p[None], gdn_conv_p[None],
            ssd_state_s[None], ssd_conv_s.reshape(1, ns, CONV_W - 1, SSD_CONV),
            gdn_state_s[None], gdn_conv_s.reshape(1, ns, CONV_W - 1, GDN_CONV))
```
